```python
import math
import jax, jax.numpy as jnp
from jax import lax
import numpy as np

D_MODEL = 1024
BATCH = 2
SEQ = 16384
DEPTH = 1
DEC_BATCH = 128
DEC_SEQ = 1
PAST_LEN = 8192
PAGE_SIZE = 128

D_RNN = D_MODEL
LRU_BLOCKS = 16
LRU_BW = D_RNN // LRU_BLOCKS
CONV_W = 4
LRU_C = 8.0
N_HEADS = 8
N_KV_HEADS = 4
HEAD_DIM = 128
GROUP = N_HEADS // N_KV_HEADS
N_IDX_HEADS = 8
IDX_DIM = 64
TOPK_MAX = 256
Q_BLOCK = 128
N_MEM = 256
X_HEADS = 4
X_HEAD_DIM = 192
N_BRANCH = 3
N_BUCKETS = 32
MAX_DISTANCE = 128
N_EXPERTS = 32
TOP_K = 4
D_FF = D_MODEL
SWIGLU_LIMIT = 7.0
SWIGLU_ALPHA = 1.702
ROW_BLOCK = 128
EPS = 1e-6
IN_WIDTHS = (D_RNN, D_RNN, N_HEADS * HEAD_DIM, N_KV_HEADS * HEAD_DIM, N_KV_HEADS * HEAD_DIM,
             N_IDX_HEADS * IDX_DIM, IDX_DIM, N_IDX_HEADS, X_HEADS * X_HEAD_DIM, N_BRANCH * D_MODEL)
D_IN = sum(IN_WIDTHS)

kernel_name = 'hawk_dsa_memory_moe_step'


def rmsnorm(x, g):
    xf = x.astype(jnp.float32)
    y = xf * lax.rsqrt(jnp.mean(xf * xf, axis=-1, keepdims=True) + EPS)
    return (y * g.astype(jnp.float32)).astype(x.dtype)


def split_proj(p):
    offs = np.cumsum(IN_WIDTHS)[:-1].tolist()
    return jnp.split(p, offs, axis=-1)


def causal_conv(x, prev, w, b):
    T = x.shape[1]
    xp = jnp.concatenate([prev.astype(x.dtype), x], axis=1)
    y = b + xp[:, 0:T] * w[0]
    for i in range(1, CONV_W):
        y = y + xp[:, i:i + T] * w[i]
    return y, xp[:, T:]


def linear_scan(a, b, h0):
    b = b.at[:, 0].add(a[:, 0] * h0)
    def comb(l, r):
        return (l[0] * r[0], r[0] * l[1] + r[1])
    _, h = lax.associative_scan(comb, (a, b), axis=1)
    return h


def rglru_branch(xr, yr, conv_prev, h0, conv_w, conv_b, lru_wa, lru_ba, lru_wx, lru_bx, lru_lambda):
    B, T, _ = xr.shape
    xc, new_conv = causal_conv(xr, conv_prev, conv_w, conv_b)
    xb = xc.reshape(B, T, LRU_BLOCKS, LRU_BW)
    r = jax.nn.sigmoid((jnp.einsum('btni,nij->btnj', xb, lru_wa).reshape(B, T, D_RNN) + lru_ba).astype(jnp.float32))
    ig = jax.nn.sigmoid((jnp.einsum('btni,nij->btnj', xb, lru_wx).reshape(B, T, D_RNN) + lru_bx).astype(jnp.float32))
    log_a = -LRU_C * r * jax.nn.softplus(-lru_lambda.astype(jnp.float32))
    a = jnp.exp(log_a)
    bterm = jnp.sqrt(-jnp.expm1(2.0 * log_a)) * ig * xc.astype(jnp.float32)
    h = linear_scan(a, bterm, h0.astype(jnp.float32))
    out = (h * jax.nn.gelu(yr.astype(jnp.float32))).astype(xr.dtype)
    return out, new_conv, h[:, -1]


def t5_bucket(dist):
    n = jnp.maximum(dist, 0)
    max_exact = N_BUCKETS // 2
    nf = jnp.maximum(n, 1).astype(jnp.float32)
    large = max_exact + (jnp.log(nf / max_exact) / math.log(MAX_DISTANCE / max_exact)
                         * (N_BUCKETS - max_exact)).astype(jnp.int32)
    large = jnp.minimum(large, N_BUCKETS - 1)
    return jnp.where(n < max_exact, n, large)


def index_scores(qi, wi, ki):
    s = jnp.einsum('bqhd,bsd->bqhs', qi, ki).astype(jnp.float32) * IDX_DIM ** -0.5
    return jnp.einsum('bqhs,bqh->bqs', jax.nn.relu(s), wi.astype(jnp.float32)) * N_IDX_HEADS ** -0.5


def select_keys(scores, q_pos, topk):
    s_pos = jnp.arange(scores.shape[-1], dtype=jnp.int32)
    masked = jnp.where(s_pos[None, None, :] <= q_pos[None, :, None], scores, -jnp.inf)
    _, idx = lax.top_k(masked, topk)
    return idx


gather_rows = jax.vmap(lambda rows, ii: rows[ii])


def sparse_attention(q, k_sel, v_sel, idx, q_pos, rel_bias):
    B, Q = q.shape[:2]
    qg = q.reshape(B, Q, N_KV_HEADS, GROUP, HEAD_DIM)
    logits = jnp.einsum('bqcgd,bqjcd->bqcgj', qg, k_sel).astype(jnp.float32) * HEAD_DIM ** -0.5
    dist = q_pos[None, :, None] - idx
    bias = rel_bias[t5_bucket(dist)].astype(jnp.float32)
    bias = bias.reshape(B, Q, -1, N_KV_HEADS, GROUP).transpose(0, 1, 3, 4, 2)
    valid = (dist >= 0)[:, :, None, None, :]
    p = jax.nn.softmax(jnp.where(valid, logits + bias, -jnp.inf), axis=-1)
    out = jnp.einsum('bqcgj,bqjcd->bqcgd', p.astype(v_sel.dtype), v_sel)
    return out.reshape(B, Q, N_HEADS * HEAD_DIM)


def dsa_prompt(q, k, v, qi, wi, ki, rel_bias):
    B, T = q.shape[:2]
    nb = T // Q_BLOCK
    topk = min(TOPK_MAX, T // 4)
    def to_blocks(a):
        return a.reshape(B, nb, Q_BLOCK, *a.shape[2:]).swapaxes(0, 1)
    def block(args):
        qb, qib, wib, start = args
        q_pos = start + jnp.arange(Q_BLOCK, dtype=jnp.int32)
        idx = select_keys(index_scores(qib, wib, ki), q_pos, topk)
        return sparse_attention(qb, gather_rows(k, idx), gather_rows(v, idx), idx, q_pos, rel_bias)
    starts = jnp.arange(nb, dtype=jnp.int32) * Q_BLOCK
    out = lax.map(block, (to_blocks(q), to_blocks(qi), to_blocks(wi), starts))
    return out.swapaxes(0, 1).reshape(B, T, N_HEADS * HEAD_DIM)


def dsa_sample(q, k_new, v_new, qi, wi, ki_new, cache_k, cache_v, cache_kidx, page_table, rel_bias):
    Bd, S = q.shape[:2]
    past = page_table.shape[1] * PAGE_SIZE
    topk = min(TOPK_MAX, (past + S) // 4)
    ki_past = cache_kidx[page_table].reshape(Bd, past, IDX_DIM)
    ki_all = jnp.concatenate([ki_past.astype(ki_new.dtype), ki_new], axis=1)
    q_pos = past + jnp.arange(S, dtype=jnp.int32)
    idx = select_keys(index_scores(qi, wi, ki_all), q_pos, topk)
    lp = jnp.minimum(idx, past - 1)
    phys = jnp.take_along_axis(page_table, (lp // PAGE_SIZE).reshape(Bd, -1), axis=1).reshape(idx.shape)
    off = lp % PAGE_SIZE
    new_i = jnp.clip(idx - past, 0, S - 1)
    is_past = (idx < past)[..., None, None]
    k_sel = jnp.where(is_past, cache_k[phys, off].astype(k_new.dtype), gather_rows(k_new, new_i))
    v_sel = jnp.where(is_past, cache_v[phys, off].astype(v_new.dtype), gather_rows(v_new, new_i))
    return sparse_attention(q, k_sel, v_sel, idx, q_pos, rel_bias)


def mem_kv(mem, mem_norm, w_mem_kv, xk_norm):
    B, M, _ = mem.shape
    kv = rmsnorm(mem, mem_norm) @ w_mem_kv
    mk, mv = jnp.split(kv, 2, axis=-1)
    mk = rmsnorm(mk.reshape(B, M, X_HEADS, X_HEAD_DIM), xk_norm)
    return mk, mv.reshape(B, M, X_HEADS, X_HEAD_DIM)


def mem_attention(xq, mk, mv):
    B, T = xq.shape[:2]
    logits = jnp.einsum('bthd,bmhd->bhtm', xq, mk).astype(jnp.float32) * X_HEAD_DIM ** -0.5
    p = jax.nn.softmax(logits, axis=-1)
    out = jnp.einsum('bhtm,bmhd->bthd', p.astype(mv.dtype), mv)
    return out.reshape(B, T, X_HEADS * X_HEAD_DIM)


def moe(x, router_w, router_b, w_gu, b_gu, w_dn, b_dn):
    T, D = x.shape
    logits = (x @ router_w + router_b).astype(jnp.float32)
    top_v, top_i = lax.top_k(logits, TOP_K)
    gates = jax.nn.softmax(top_v, axis=-1)
    A = T * TOP_K
    e_flat = top_i.reshape(A)
    tok_flat = jnp.repeat(jnp.arange(T, dtype=jnp.int32), TOP_K)
    g_flat = gates.reshape(A)
    order = jnp.argsort(e_flat)
    e_s, tok_s, g_s = e_flat[order], tok_flat[order], g_flat[order]
    counts = jnp.bincount(e_flat, length=N_EXPERTS)
    start = jnp.cumsum(counts) - counts
    padded = (counts + ROW_BLOCK - 1) // ROW_BLOCK * ROW_BLOCK
    pend = jnp.cumsum(padded)
    pstart = pend - padded
    dest = pstart[e_s] + (jnp.arange(A, dtype=jnp.int32) - start[e_s])
    nb = (A + ROW_BLOCK - 1) // ROW_BLOCK + N_EXPERTS
    xbuf = jnp.zeros((nb * ROW_BLOCK, D), x.dtype).at[dest].set(x[tok_s])
    blk_e = jnp.minimum(jnp.searchsorted(pend, jnp.arange(nb, dtype=jnp.int32) * ROW_BLOCK, side='right'), N_EXPERTS - 1)
    def expert(args):
        xb, e = args
        hgu = (xb @ w_gu[e] + b_gu[e]).astype(jnp.float32)
        gate, up = jnp.split(hgu, 2, axis=-1)
        gate = jnp.minimum(gate, SWIGLU_LIMIT)
        up = jnp.clip(up, -SWIGLU_LIMIT, SWIGLU_LIMIT)
        act = (up + 1.0) * gate * jax.nn.sigmoid(SWIGLU_ALPHA * gate)
        return act.astype(xb.dtype) @ w_dn[e] + b_dn[e]
    ybuf = lax.map(expert, (xbuf.reshape(nb, ROW_BLOCK, D), blk_e)).reshape(nb * ROW_BLOCK, D)
    y_assign = ybuf[dest] * g_s[:, None].astype(ybuf.dtype)
    return jax.ops.segment_sum(y_assign, tok_s, num_segments=T)


def layer(x, conv_prev, h0, mem_k, mem_v, attend, W):
    B, T, _ = x.shape
    h = rmsnorm(x, W['norm_mix'])
    xr, yr, q, k, v, qi, ki, wi, xq, g = split_proj(h @ W['w_in'])
    rnn_out, new_conv, h_last = rglru_branch(xr, yr, conv_prev, h0, W['conv_w'], W['conv_b'], W['lru_wa'],
                                             W['lru_ba'], W['lru_wx'], W['lru_bx'], W['lru_lambda'])
    q = rmsnorm(q.reshape(B, T, N_HEADS, HEAD_DIM), W['q_norm'])
    k = rmsnorm(k.reshape(B, T, N_KV_HEADS, HEAD_DIM), W['k_norm'])
    v = v.reshape(B, T, N_KV_HEADS, HEAD_DIM)
    qi = qi.reshape(B, T, N_IDX_HEADS, IDX_DIM)
    attn_out = attend(q, k, v, qi, wi, ki)
    xq = rmsnorm(xq.reshape(B, T, X_HEADS, X_HEAD_DIM), W['xq_norm'])
    mem_out = mem_attention(xq, mem_k, mem_v)
    gates = jax.nn.sigmoid(g.astype(jnp.float32)).reshape(B, T, N_BRANCH, D_MODEL)
    mixed = (gates[:, :, 0] * (rnn_out @ W['w_br_rnn']).astype(jnp.float32)
             + gates[:, :, 1] * (attn_out @ W['w_br_attn']).astype(jnp.float32)
             + gates[:, :, 2] * (mem_out @ W['w_br_mem']).astype(jnp.float32))
    x = x + mixed.astype(x.dtype) @ W['w_out']
    hf = rmsnorm(x, W['norm_ffn']).reshape(B * T, D_MODEL)
    y = x + moe(hf, W['router_w'], W['router_b'], W['exp_w_gu'], W['exp_b_gu'],
                W['exp_w_down'], W['exp_b_down']).reshape(B, T, D_MODEL)
    return y, new_conv, h_last, k, v, ki


def setup_inputs(seed: int = 0) -> dict:
    key = jax.random.key(seed)
    ks = jax.random.split(key, 40)
    f32 = jnp.float32
    def nrm(k, shape, scale=1.0):
        return jax.random.normal(k, shape, f32) * scale
    n_pages = PAST_LEN // PAGE_SIZE
    n_used = DEC_BATCH * n_pages
    n_phys = n_used + max(1, n_used // 4)
    page_table = jax.random.permutation(ks[0], n_phys)[:n_used].reshape(DEC_BATCH, n_pages).astype(jnp.int32)
    u = jax.random.uniform(ks[1], (D_RNN,), f32, 0.9, 0.999)
    a = u ** (1.0 / LRU_C)
    lru_lambda = jnp.log(a) - jnp.log1p(-a)
    return {
        'x_prompt': nrm(ks[2], (BATCH, SEQ, D_MODEL)),
        'x_sample': nrm(ks[3], (DEC_BATCH, DEC_SEQ, D_MODEL)),
        'cache_k': nrm(ks[4], (n_phys, PAGE_SIZE, N_KV_HEADS, HEAD_DIM)),
        'cache_v': nrm(ks[5], (n_phys, PAGE_SIZE, N_KV_HEADS, HEAD_DIM)),
        'cache_kidx': nrm(ks[6], (n_phys, PAGE_SIZE, IDX_DIM)),
        'cache_mem_k': nrm(ks[7], (DEC_BATCH, N_MEM, X_HEADS, X_HEAD_DIM)),
        'cache_mem_v': nrm(ks[8], (DEC_BATCH, N_MEM, X_HEADS, X_HEAD_DIM)),
        'state_conv': nrm(ks[9], (DEC_BATCH, CONV_W - 1, D_RNN)),
        'state_rglru': nrm(ks[10], (DEC_BATCH, D_RNN), 0.5),
        'page_table': page_table,
        'mem_prompt': nrm(ks[11], (BATCH, N_MEM, D_MODEL)),
        'norm_mix': 1.0 + nrm(ks[12], (D_MODEL,), 0.02),
        'w_in': nrm(ks[13], (D_MODEL, D_IN), D_MODEL ** -0.5),
        'conv_w': nrm(ks[14], (CONV_W, D_RNN), 0.5),
        'conv_b': nrm(ks[15], (D_RNN,), 0.02),
        'lru_wa': nrm(ks[16], (LRU_BLOCKS, LRU_BW, LRU_BW), LRU_BW ** -0.5),
        'lru_ba': nrm(ks[17], (D_RNN,), 0.1),
        'lru_wx': nrm(ks[18], (LRU_BLOCKS, LRU_BW, LRU_BW), LRU_BW ** -0.5),
        'lru_bx': nrm(ks[19], (D_RNN,), 0.1),
        'lru_lambda': lru_lambda,
        'q_norm': 1.0 + nrm(ks[20], (HEAD_DIM,), 0.02),
        'k_norm': 1.0 + nrm(ks[21], (HEAD_DIM,), 0.02),
        'rel_bias': nrm(ks[22], (N_BUCKETS, N_HEADS), 0.5),
        'mem_norm': 1.0 + nrm(ks[23], (D_MODEL,), 0.02),
        'w_mem_kv': nrm(ks[24], (D_MODEL, 2 * X_HEADS * X_HEAD_DIM), D_MODEL ** -0.5),
        'xq_norm': 1.0 + nrm(ks[25], (X_HEAD_DIM,), 0.02),
        'xk_norm': 1.0 + nrm(ks[26], (X_HEAD_DIM,), 0.02),
        'w_br_rnn': nrm(ks[27], (D_RNN, D_MODEL), D_RNN ** -0.5),
        'w_br_attn': nrm(ks[28], (N_HEADS * HEAD_DIM, D_MODEL), (N_HEADS * HEAD_DIM) ** -0.5),
        'w_br_mem': nrm(ks[29], (X_HEADS * X_HEAD_DIM, D_MODEL), (X_HEADS * X_HEAD_DIM) ** -0.5),
        'w_out': nrm(ks[30], (D_MODEL, D_MODEL), D_MODEL ** -0.5),
        'norm_ffn': 1.0 + nrm(ks[31], (D_MODEL,), 0.02),
        'router_w': nrm(ks[32], (D_MODEL, N_EXPERTS), D_MODEL ** -0.5),
        'router_b': nrm(ks[33], (N_EXPERTS,), 0.01),
        'exp_w_gu': nrm(ks[34], (N_EXPERTS, D_MODEL, 2 * D_FF), D_MODEL ** -0.5),
        'exp_b_gu': nrm(ks[35], (N_EXPERTS, 2 * D_FF), 0.01),
        'exp_w_down': nrm(ks[36], (N_EXPERTS, D_FF, D_MODEL), D_FF ** -0.5),
        'exp_b_down': nrm(ks[37], (N_EXPERTS, D_MODEL), 0.01),
    }


def reference(x_prompt, x_sample, cache_k, cache_v, cache_kidx, cache_mem_k, cache_mem_v, state_conv,
              state_rglru, page_table, mem_prompt, norm_mix, w_in, conv_w, conv_b, lru_wa, lru_ba, lru_wx,
              lru_bx, lru_lambda, q_norm, k_norm, rel_bias, mem_norm, w_mem_kv, xq_norm, xk_norm, w_br_rnn,
              w_br_attn, w_br_mem, w_out, norm_ffn, router_w, router_b, exp_w_gu, exp_b_gu, exp_w_down,
              exp_b_down):
    W = dict(norm_mix=norm_mix, w_in=w_in, conv_w=conv_w, conv_b=conv_b, lru_wa=lru_wa, lru_ba=lru_ba,
             lru_wx=lru_wx, lru_bx=lru_bx, lru_lambda=lru_lambda, q_norm=q_norm, k_norm=k_norm,
             xq_norm=xq_norm, w_br_rnn=w_br_rnn, w_br_attn=w_br_attn, w_br_mem=w_br_mem, w_out=w_out,
             norm_ffn=norm_ffn, router_w=router_w, router_b=router_b, exp_w_gu=exp_w_gu, exp_b_gu=exp_b_gu,
             exp_w_down=exp_w_down, exp_b_down=exp_b_down)
    B = x_prompt.shape[0]
    mk_p, mv_p = mem_kv(mem_prompt, mem_norm, w_mem_kv, xk_norm)
    conv0 = jnp.zeros((B, CONV_W - 1, D_RNN), x_prompt.dtype)
    h00 = jnp.zeros((B, D_RNN), jnp.float32)
    def attend_prompt(q, k, v, qi, wi, ki):
        return dsa_prompt(q, k, v, qi, wi, ki, rel_bias)
    y_prompt, conv_p, rg_p, k_p, v_p, ki_p = layer(x_prompt, conv0, h00, mk_p, mv_p, attend_prompt, W)
    def attend_sample(q, k, v, qi, wi, ki):
        return dsa_sample(q, k, v, qi, wi, ki, cache_k, cache_v, cache_kidx, page_table, rel_bias)
    y_sample, conv_s, rg_s, k_s, v_s, ki_s = layer(x_sample, state_conv, state_rglru, cache_mem_k,
                                                   cache_mem_v, attend_sample, W)
    return (y_prompt, y_sample, k_p, v_p, ki_p, mk_p, mv_p, conv_p, rg_p, k_s, v_s, ki_s, conv_s, rg_s)
```

```python
import functools
import math

import numpy as np
import jax
import jax.numpy as jnp
from jax import lax
from jax.experimental import pallas as pl
from jax.experimental.pallas import tpu as pltpu

F32 = jnp.float32
BF16 = jnp.bfloat16
I32 = jnp.int32

D_MODEL = 1024
D_RNN = 1024
LRU_BLOCKS = 16
LRU_BW = 64
CONV_W = 4
LRU_C = 8.0
N_HEADS = 8
N_KV_HEADS = 4
HEAD_DIM = 128
N_IDX_HEADS = 8
IDX_DIM = 64
TOPK = 256
PAGE_SIZE = 128
N_MEM = 256
X_HEADS = 4
X_HEAD_DIM = 192
X_HEAD_PAD = 256
N_BUCKETS = 32
MAX_DISTANCE = 128
N_EXPERTS = 32
TOP_K = 4
D_FF = 1024
SWIGLU_LIMIT = 7.0
SWIGLU_ALPHA = 1.702
EPS = 1e-6

INT_MIN = -2 ** 31
NEG = -1e30
VMEM_LIMIT = 56 * 1024 * 1024

DSA_TQ = 256
DSA_KB = 1024
MOE_RB = 256


def _cparams(sem):
    return pltpu.CompilerParams(dimension_semantics=sem, vmem_limit_bytes=VMEM_LIMIT)


def _sigmoid(x):
    return 1.0 / (1.0 + jnp.exp(-x))


def _gelu_tanh(x):
    return 0.5 * x * (1.0 + jnp.tanh(0.7978845608028654 * (x + 0.044715 * x * x * x)))


def _head_norm(y, gain, hd_pad, hd_true):
    parts = []
    for s in range(0, y.shape[1], hd_pad):
        ys = y[:, s:s + hd_pad]
        ms = jnp.sum(ys * ys, axis=-1, keepdims=True) * (1.0 / hd_true)
        parts.append(ys * lax.rsqrt(ms + EPS) * gain[:, s:s + hd_pad])
    return parts[0] if len(parts) == 1 else jnp.concatenate(parts, axis=-1)


def _proj_body(x_ref, g_ref, w_ref, *refs, mode, hd_pad, hd_true, n_out):
    if mode == "headnorm":
        gain_ref, refs = refs[0], refs[1:]
    outs, h_ref = refs[:n_out], refs[n_out]

    @pl.when(pl.program_id(1) == 0)
    def _():
        x = x_ref[...]
        ms = jnp.mean(x * x, axis=-1, keepdims=True)
        h_ref[...] = (x * lax.rsqrt(ms + EPS) * g_ref[...]).astype(BF16)

    y = jnp.dot(h_ref[...], w_ref[...], preferred_element_type=F32)
    if mode == "headnorm":
        y = _head_norm(y, gain_ref[...], hd_pad, hd_true)
    for o in outs:
        o[...] = y.astype(o.dtype)


def _norm_proj(x, gain, w, out_dtypes, *, tn, mode="plain", head_gain=None, hd_pad=128, hd_true=128):
    M, K = x.shape
    N = w.shape[1]
    tm = min(512, M)
    assert M % tm == 0 and N % tn == 0
    in_specs = [pl.BlockSpec((tm, K), lambda i, j: (i, 0)),
                pl.BlockSpec((1, K), lambda i, j: (0, 0)),
                pl.BlockSpec((K, tn), lambda i, j: (0, j))]
    args = [x, gain.reshape(1, K), w]
    if mode == "headnorm":
        in_specs.append(pl.BlockSpec((1, tn), lambda i, j: (0, j)))
        args.append(head_gain.reshape(1, N))
    body = functools.partial(_proj_body, mode=mode, hd_pad=hd_pad, hd_true=hd_true, n_out=len(out_dtypes))
    return pl.pallas_call(
        body,
        grid=(M // tm, N // tn),
        in_specs=in_specs,
        out_specs=[pl.BlockSpec((tm, tn), lambda i, j: (i, j)) for _ in out_dtypes],
        out_shape=[jax.ShapeDtypeStruct((M, N), dt) for dt in out_dtypes],
        scratch_shapes=[pltpu.VMEM((tm, K), BF16)],
        compiler_params=_cparams(("parallel", "arbitrary")),
        name="norm_proj_" + mode,
    )(*args)


def _softplus(z):
    return jnp.maximum(z, 0.0) + jnp.log1p(jnp.exp(-jnp.abs(z)))


def _lru_terms(xc, wg_ref, ba, bx, lam, a_out, b_out):
    sp = _softplus(-lam)
    for gi in range(4):
        sl = slice(gi * 256, (gi + 1) * 256)
        xg = xc[:, sl]
        z = jnp.dot(xg.astype(BF16), wg_ref[gi], preferred_element_type=F32)
        r = _sigmoid(z[:, :256] + ba[:, sl])
        ig = _sigmoid(z[:, 256:] + bx[:, sl])
        log_a = -LRU_C * r * sp[:, sl]
        a_out[:, sl] = jnp.exp(log_a)
        b_out[:, sl] = jnp.sqrt(1.0 - jnp.exp(2.0 * log_a)) * ig * xg


def _rglru_prompt_body(xr_ref, yr_ref, cw_ref, cb_ref, wg_ref, ba_ref, bx_ref, lam_ref,
                       out_ref, hl_ref, xp_ref, a_ref, b_ref, h_ref, hc_ref, *, tc):
    t = pl.program_id(1)

    @pl.when(t == 0)
    def _():
        xp_ref[0:8, :] = jnp.zeros((8, D_RNN), F32)
        hc_ref[...] = jnp.zeros_like(hc_ref)

    @pl.when(t > 0)
    def _():
        xp_ref[0:8, :] = xp_ref[tc:tc + 8, :]

    xp_ref[8:8 + tc, :] = xr_ref[0]
    cw = cw_ref[...]
    xc = cb_ref[...] + cw[0:1] * xp_ref[5:5 + tc, :]
    xc = xc + cw[1:2] * xp_ref[6:6 + tc, :]
    xc = xc + cw[2:3] * xp_ref[7:7 + tc, :]
    xc = xc + cw[3:4] * xp_ref[8:8 + tc, :]
    _lru_terms(xc, wg_ref, ba_ref[...], bx_ref[...], lam_ref[...], a_ref, b_ref)

    def step(i, h):
        h = a_ref[pl.ds(i, 1), :] * h + b_ref[pl.ds(i, 1), :]
        h_ref[pl.ds(i, 1), :] = h
        return h

    h = lax.fori_loop(0, tc, step, hc_ref[0:1, :], unroll=8)
    hc_ref[0:1, :] = h
    out_ref[0] = (h_ref[...] * _gelu_tanh(yr_ref[0])).astype(out_ref.dtype)

    @pl.when(t == pl.num_programs(1) - 1)
    def _():
        hl_ref[0] = h


def _rglru_prompt(xy, conv_w, conv_b, wg, ba, bx, lam):
    B, T, _ = xy.shape
    tc = min(256, T)
    vec = lambda a: a.reshape(1, D_RNN)
    full = lambda shape: pl.BlockSpec(shape, lambda b, t: (0,) * len(shape))
    return pl.pallas_call(
        functools.partial(_rglru_prompt_body, tc=tc),
        grid=(B, T // tc),
        in_specs=[pl.BlockSpec((1, tc, D_RNN), lambda b, t: (b, t, 0)),
                  pl.BlockSpec((1, tc, D_RNN), lambda b, t: (b, t, 1)),
                  full((CONV_W, D_RNN)), full((1, D_RNN)), full((4, 256, 512)),
                  full((1, D_RNN)), full((1, D_RNN)), full((1, D_RNN))],
        out_specs=[pl.BlockSpec((1, tc, D_RNN), lambda b, t: (b, t, 0)),
                   pl.BlockSpec((1, 1, D_RNN), lambda b, t: (b, 0, 0))],
        out_shape=[jax.ShapeDtypeStruct((B, T, D_RNN), BF16),
                   jax.ShapeDtypeStruct((B, 1, D_RNN), F32)],
        scratch_shapes=[pltpu.VMEM((tc + 8, D_RNN), F32), pltpu.VMEM((tc, D_RNN), F32),
                        pltpu.VMEM((tc, D_RNN), F32), pltpu.VMEM((tc, D_RNN), F32),
                        pltpu.VMEM((8, D_RNN), F32)],
        compiler_params=_cparams(("parallel", "arbitrary")),
        name="rglru_prompt",
    )(xy, xy, conv_w, vec(conv_b), wg, vec(ba), vec(bx), vec(lam))


def _rglru_step_body(xr_ref, yr_ref, prev_ref, h0_ref, cw_ref, cb_ref, wg_ref, ba_ref, bx_ref, lam_ref,
                     out_ref, hn_ref, a_ref, b_ref):
    cw = cw_ref[...]
    xc = cb_ref[...] + cw[0:1] * prev_ref[0] + cw[1:2] * prev_ref[1] + cw[2:3] * prev_ref[2] + cw[3:4] * xr_ref[...]
    _lru_terms(xc, wg_ref, ba_ref[...], bx_ref[...], lam_ref[...], a_ref, b_ref)
    h = a_ref[...] * h0_ref[...] + b_ref[...]
    hn_ref[...] = h
    out_ref[...] = (h * _gelu_tanh(yr_ref[...])).astype(out_ref.dtype)


def _rglru_step(xy, prev, h0, conv_w, conv_b, wg, ba, bx, lam):
    R = xy.shape[0]
    vec = lambda a: a.reshape(1, D_RNN)
    full = lambda shape: pl.BlockSpec(shape, lambda i: (0,) * len(shape))
    return pl.pallas_call(
        _rglru_step_body,
        grid=(1,),
        in_specs=[pl.BlockSpec((R, D_RNN), lambda i: (0, 0)), pl.BlockSpec((R, D_RNN), lambda i: (0, 1)),
                  full((CONV_W - 1, R, D_RNN)), full((R, D_RNN)),
                  full((CONV_W, D_RNN)), full((1, D_RNN)), full((4, 256, 512)),
                  full((1, D_RNN)), full((1, D_RNN)), full((1, D_RNN))],
        out_specs=[full((R, D_RNN)), full((R, D_RNN))],
        out_shape=[jax.ShapeDtypeStruct((R, D_RNN), BF16), jax.ShapeDtypeStruct((R, D_RNN), F32)],
        scratch_shapes=[pltpu.VMEM((R, D_RNN), F32), pltpu.VMEM((R, D_RNN), F32)],
        compiler_params=_cparams(("arbitrary",)),
        name="rglru_step",
    )(xy, xy, prev, h0, conv_w, vec(conv_b), wg, vec(ba), vec(bx), vec(lam))


def _block_diag_gates(wa, wx):
    def bd(w):
        w4 = w.reshape(4, 4, LRU_BW, LRU_BW)
        return jnp.einsum('gaij,ab->gaibj', w4, jnp.eye(4, dtype=w.dtype)).reshape(4, 256, 256)
    return jnp.concatenate([bd(wa), bd(wx)], axis=-1).astype(BF16)


def _sortable_key(s):
    bits = pltpu.bitcast(s + 0.0, I32)
    return jnp.where(bits >= 0, bits, bits ^ jnp.int32(0x7FFFFFFF))


def _topk_threshold(count_ge, rows):
    cnt0 = count_ge(jnp.zeros((rows, 1), I32))
    base = jnp.where(cnt0 >= TOPK, jnp.int32(0), jnp.int32(INT_MIN))

    def bit_body(i, base):
        cand = base | jnp.left_shift(jnp.int32(1), jnp.int32(30) - i)
        return jnp.where(count_ge(cand) >= TOPK, cand, base)

    return lax.fori_loop(0, 31, bit_body, base)


def _attend_block(q_ref, k, v, mask, bias_of_head, m_ref, l_ref, acc_ref):
    scale = HEAD_DIM ** -0.5
    for h in range(N_HEADS):
        c = h // (N_HEADS // N_KV_HEADS)
        hs = slice(h * HEAD_DIM, (h + 1) * HEAD_DIM)
        cs = slice(c * HEAD_DIM, (c + 1) * HEAD_DIM)
        s = lax.dot_general(q_ref[0, :, hs], k[:, cs], (((1,), (1,)), ((), ())),
                            preferred_element_type=F32)
        s = jnp.where(mask, s * scale + bias_of_head(h), NEG)
        m_old = m_ref[h]
        m_new = jnp.maximum(m_old, jnp.max(s, axis=-1, keepdims=True))
        alpha = jnp.exp(m_old - m_new)
        p = jnp.exp(s - m_new)
        l_ref[h] = alpha * l_ref[h] + jnp.sum(p, axis=-1, keepdims=True)
        acc_ref[:, hs] = alpha * acc_ref[:, hs] + jnp.dot(p.astype(BF16), v[:, cs],
                                                          preferred_element_type=F32)
        m_ref[h] = m_new


def _dsa_prompt_body(bfar_ref, qi_ref, wi_ref, kit_ref, q_ref, kf_ref, vf_ref, kp_ref, vp_ref,
                     kd_ref, vd_ref, tbd_ref, tbp_ref, out_ref,
                     s_ref, thr_ref, m_ref, l_ref, acc_ref, *, tq, kb):
    qb = pl.program_id(1)
    st = pl.program_id(2)
    t0 = qb * tq
    far_len = jnp.maximum(qb - 1, 0) * tq
    n_far = (far_len + kb - 1) // kb

    @pl.when(st == 0)
    def _():
        m_ref[...] = jnp.full(m_ref.shape, NEG, F32)
        l_ref[...] = jnp.zeros_like(l_ref)
        acc_ref[...] = jnp.zeros_like(acc_ref)

        nkc = (t0 + tq + kb - 1) // kb
        wi = wi_ref[0]
        row = t0 + lax.broadcasted_iota(I32, (tq, kb), 0)
        lane = lax.broadcasted_iota(I32, (tq, kb), 1)

        def score_chunk(c, carry):
            off = pl.multiple_of(c * kb, kb)
            kic = kit_ref[0, :, pl.ds(off, kb)]
            acc = jnp.zeros((tq, kb), F32)
            for h in range(N_IDX_HEADS):
                s = jnp.dot(qi_ref[0, h], kic, preferred_element_type=F32)
                acc = acc + jnp.maximum(s, 0.0) * wi[:, h:h + 1]
            key = jnp.where(off + lane <= row, _sortable_key(acc), jnp.int32(INT_MIN))
            s_ref[:, pl.ds(off, kb)] = key
            return carry

        lax.fori_loop(0, nkc, score_chunk, 0)

        def count_ge(cand):
            def body(c, part):
                off = pl.multiple_of(c * kb, kb)
                ge = (s_ref[:, pl.ds(off, kb)] >= cand).astype(I32)
                for j in range(kb // 128):
                    part = part + ge[:, j * 128:(j + 1) * 128]
                return part
            part = lax.fori_loop(0, nkc, body, jnp.zeros((tq, 128), I32))
            return jnp.sum(part, axis=-1, keepdims=True)

        thr = jnp.maximum(_topk_threshold(count_ge, tq), jnp.int32(INT_MIN + 1))
        thr_ref[...] = thr

        @pl.when(qb > 0)
        def _():
            off = pl.multiple_of(t0 - tq, tq)
            mask = s_ref[:, pl.ds(off, tq)] >= thr
            _attend_block(q_ref, kp_ref[0], vp_ref[0], mask, lambda h: tbp_ref[h], m_ref, l_ref, acc_ref)

        off = pl.multiple_of(t0, tq)
        mask = s_ref[:, pl.ds(off, tq)] >= thr
        _attend_block(q_ref, kd_ref[0], vd_ref[0], mask, lambda h: tbd_ref[h], m_ref, l_ref, acc_ref)

    @pl.when(jnp.logical_and(st > 0, st - 1 < n_far))
    def _():
        off = pl.multiple_of((st - 1) * kb, kb)
        col = off + lax.broadcasted_iota(I32, (tq, kb), 1)
        mask = jnp.logical_and(s_ref[:, pl.ds(off, kb)] >= thr_ref[...], col < far_len)
        _attend_block(q_ref, kf_ref[0], vf_ref[0], mask, lambda h: bfar_ref[h], m_ref, l_ref, acc_ref)

    @pl.when(st == pl.num_programs(2) - 1)
    def _():
        for h in range(N_HEADS):
            hs = slice(h * HEAD_DIM, (h + 1) * HEAD_DIM)
            out_ref[0, :, hs] = (acc_ref[:, hs] / l_ref[h]).astype(out_ref.dtype)


def _t5_bucket(dist):
    n = jnp.maximum(dist, 0)
    max_exact = N_BUCKETS // 2
    nf = jnp.maximum(n, 1).astype(F32)
    large = max_exact + (jnp.log(nf / max_exact) / math.log(MAX_DISTANCE / max_exact)
                         * (N_BUCKETS - max_exact)).astype(I32)
    large = jnp.minimum(large, N_BUCKETS - 1)
    return jnp.where(n < max_exact, n, large)


_FAR_DIST = int(math.ceil((N_BUCKETS // 2) * (MAX_DISTANCE / (N_BUCKETS // 2)) **
                          ((N_BUCKETS - 1 - N_BUCKETS // 2) / (N_BUCKETS - N_BUCKETS // 2)))) + 1


def _dsa_prompt(q, k, v, qi_h, wi, kit, rel_bias):
    B, T, _ = q.shape
    tq = min(DSA_TQ, T)
    kb = min(DSA_KB, T)
    assert T % tq == 0 and T % kb == 0 and kb % tq == 0 and tq >= _FAR_DIST and T >= 4 * TOPK
    nq = T // tq
    n_far_max = max(1, ((nq - 2) * tq + kb - 1) // kb) if nq > 1 else 1
    d = jnp.arange(tq, dtype=I32)[:, None] - jnp.arange(tq, dtype=I32)[None, :]
    tb_diag = rel_bias[_t5_bucket(d)].astype(F32).transpose(2, 0, 1)
    tb_prev = rel_bias[_t5_bucket(d + tq)].astype(F32).transpose(2, 0, 1)
    b_far = rel_bias[N_BUCKETS - 1].astype(F32)

    def far_idx(b, qb, st):
        far_len = jnp.maximum(qb - 1, 0) * tq
        n_far = (far_len + kb - 1) // kb
        return (b, jnp.clip(st - 1, 0, jnp.maximum(n_far - 1, 0)), 0)

    kv_far = pl.BlockSpec((1, kb, N_KV_HEADS * HEAD_DIM), far_idx)
    kv_prev = pl.BlockSpec((1, tq, N_KV_HEADS * HEAD_DIM), lambda b, qb, st: (b, jnp.maximum(qb - 1, 0), 0))
    kv_diag = pl.BlockSpec((1, tq, N_KV_HEADS * HEAD_DIM), lambda b, qb, st: (b, qb, 0))
    tile = pl.BlockSpec((N_HEADS, tq, tq), lambda b, qb, st: (0, 0, 0))
    return pl.pallas_call(
        functools.partial(_dsa_prompt_body, tq=tq, kb=kb),
        grid=(B, nq, n_far_max + 1),
        in_specs=[pl.BlockSpec(memory_space=pltpu.SMEM),
                  pl.BlockSpec((1, N_IDX_HEADS, tq, IDX_DIM), lambda b, qb, st: (b, 0, qb, 0)),
                  pl.BlockSpec((1, tq, N_IDX_HEADS), lambda b, qb, st: (b, qb, 0)),
                  pl.BlockSpec((1, IDX_DIM, T), lambda b, qb, st: (b, 0, 0)),
                  pl.BlockSpec((1, tq, N_HEADS * HEAD_DIM), lambda b, qb, st: (b, qb, 0)),
                  kv_far, kv_far, kv_prev, kv_prev, kv_diag, kv_diag, tile, tile],
        out_specs=pl.BlockSpec((1, tq, N_HEADS * HEAD_DIM), lambda b, qb, st: (b, qb, 0)),
        out_shape=jax.ShapeDtypeStruct((B, T, N_HEADS * HEAD_DIM), BF16),
        scratch_shapes=[pltpu.VMEM((tq, T), I32), pltpu.VMEM((tq, 1), I32),
                        pltpu.VMEM((N_HEADS, tq, 1), F32), pltpu.VMEM((N_HEADS, tq, 1), F32),
                        pltpu.VMEM((tq, N_HEADS * HEAD_DIM), F32)],
        compiler_params=_cparams(("parallel", "arbitrary", "arbitrary")),
        name="dsa_prompt",
    )(b_far, qi_h, wi, kit, q, k, v, k, v, k, v, tb_diag, tb_prev)


def _mem_attn_body(xq_ref, mk_ref, mv_ref, out_ref):
    scale = X_HEAD_DIM ** -0.5
    for h in range(X_HEADS):
        hs = slice(h * X_HEAD_PAD, (h + 1) * X_HEAD_PAD)
        s = lax.dot_general(xq_ref[0, :, hs], mk_ref[0, :, hs], (((1,), (1,)), ((), ())),
                            preferred_element_type=F32) * scale
        m = jnp.max(s, axis=-1, keepdims=True)
        p = jnp.exp(s - m)
        p = p / jnp.sum(p, axis=-1, keepdims=True)
        out_ref[0, :, hs] = jnp.dot(p.astype(BF16), mv_ref[0, :, hs],
                                    preferred_element_type=F32).astype(out_ref.dtype)


def _mem_attn(xq, mk, mv):
    B, T, W = xq.shape
    M = mk.shape[1]
    tm = min(512, T)
    return pl.pallas_call(
        _mem_attn_body,
        grid=(B, T // tm),
        in_specs=[pl.BlockSpec((1, tm, W), lambda b, i: (b, i, 0)),
                  pl.BlockSpec((1, M, W), lambda b, i: (b, 0, 0)),
                  pl.BlockSpec((1, M, W), lambda b, i: (b, 0, 0))],
        out_specs=pl.BlockSpec((1, tm, W), lambda b, i: (b, i, 0)),
        out_shape=jax.ShapeDtypeStruct((B, T, W), BF16),
        compiler_params=_cparams(("parallel", "parallel")),
        name="mem_attn",
    )(xq, mk, mv)


def _merge_body(x_ref, rnn_ref, att_ref, mem_ref, g_ref, wr_ref, wa_ref, wm_ref, wo_ref, gn_ref,
                x1_ref, hf_ref):
    mixed = _sigmoid(g_ref[:, 0:D_MODEL]) * jnp.dot(rnn_ref[...], wr_ref[...], preferred_element_type=F32)
    mixed = mixed + _sigmoid(g_ref[:, D_MODEL:2 * D_MODEL]) * jnp.dot(att_ref[...], wa_ref[...],
                                                                     preferred_element_type=F32)
    mixed = mixed + _sigmoid(g_ref[:, 2 * D_MODEL:3 * D_MODEL]) * jnp.dot(mem_ref[...], wm_ref[...],
                                                                         preferred_element_type=F32)
    x1 = x_ref[...] + jnp.dot(mixed.astype(BF16), wo_ref[...], preferred_element_type=F32)
    x1_ref[...] = x1
    ms = jnp.mean(x1 * x1, axis=-1, keepdims=True)
    hf_ref[...] = (x1 * lax.rsqrt(ms + EPS) * gn_ref[...]).astype(BF16)


def _merge(x, rnn, att, mem, g, wr, wa, wm, wo, gain):
    M = x.shape[0]
    tm = min(256, M)
    row = lambda w: pl.BlockSpec((tm, w), lambda i: (i, 0))
    full = lambda a: pl.BlockSpec(a.shape, lambda i: (0,) * a.ndim)
    gain = gain.reshape(1, D_MODEL)
    return pl.pallas_call(
        _merge_body,
        grid=(M // tm,),
        in_specs=[row(D_MODEL), row(rnn.shape[1]), row(att.shape[1]), row(mem.shape[1]), row(3 * D_MODEL),
                  full(wr), full(wa), full(wm), full(wo), full(gain)],
        out_specs=[row(D_MODEL), row(D_MODEL)],
        out_shape=[jax.ShapeDtypeStruct((M, D_MODEL), F32), jax.ShapeDtypeStruct((M, D_MODEL), BF16)],
        compiler_params=_cparams(("parallel",)),
        name="merge",
    )(x, rnn, att, mem, g, wr, wa, wm, wo, gain)


def _router_body(h_ref, w_ref, b_ref, idx_ref, gate_ref):
    logits = jnp.dot(h_ref[...], w_ref[...], preferred_element_type=F32) + b_ref[...]
    lane = lax.broadcasted_iota(I32, logits.shape, 1)
    vals, idxs = [], []
    for _ in range(TOP_K):
        m = jnp.max(logits, axis=-1, keepdims=True)
        i = jnp.min(jnp.where(logits == m, lane, jnp.int32(128)), axis=-1, keepdims=True)
        vals.append(m)
        idxs.append(i)
        logits = jnp.where(lane == i, -jnp.inf, logits)
    es = [jnp.exp(v - vals[0]) for v in vals]
    tot = es[0] + es[1] + es[2] + es[3]
    io = jnp.zeros(lane.shape, I32)
    go = jnp.zeros(lane.shape, F32)
    for k in range(TOP_K):
        io = jnp.where(lane == k, idxs[k], io)
        go = jnp.where(lane == k, es[k] / tot, go)
    idx_ref[...] = io
    gate_ref[...] = go


def _router(hf, rw, rb):
    M = hf.shape[0]
    tm = min(512, M)
    return pl.pallas_call(
        _router_body,
        grid=(M // tm,),
        in_specs=[pl.BlockSpec((tm, D_MODEL), lambda i: (i, 0)),
                  pl.BlockSpec((D_MODEL, 128), lambda i: (0, 0)),
                  pl.BlockSpec((1, 128), lambda i: (0, 0))],
        out_specs=[pl.BlockSpec((tm, 128), lambda i: (i, 0)), pl.BlockSpec((tm, 128), lambda i: (i, 0))],
        out_shape=[jax.ShapeDtypeStruct((M, 128), I32), jax.ShapeDtypeStruct((M, 128), F32)],
        compiler_params=_cparams(("parallel",)),
        name="router",
    )(hf, rw, rb)


def _expert_body(be_ref, x_ref, wgu_ref, bgu_ref, wdn_ref, bdn_ref, y_ref):
    hgu = jnp.dot(x_ref[...], wgu_ref[0], preferred_element_type=F32) + bgu_ref[0]
    gate = jnp.minimum(hgu[:, :D_FF], SWIGLU_LIMIT)
    up = jnp.clip(hgu[:, D_FF:], -SWIGLU_LIMIT, SWIGLU_LIMIT)
    act = (up + 1.0) * gate * _sigmoid(SWIGLU_ALPHA * gate)
    y_ref[...] = jnp.dot(act.astype(BF16), wdn_ref[0], preferred_element_type=F32) + bdn_ref[0]


def _experts(xbuf, blk_e, wgu, bgu, wdn, bdn):
    nb = blk_e.shape[0]
    rb = xbuf.shape[0] // nb
    grid_spec = pltpu.PrefetchScalarGridSpec(
        num_scalar_prefetch=1,
        grid=(nb,),
        in_specs=[pl.BlockSpec((rb, D_MODEL), lambda i, be: (i, 0)),
                  pl.BlockSpec((1, D_MODEL, 2 * D_FF), lambda i, be: (be[i], 0, 0)),
                  pl.BlockSpec((1, 1, 2 * D_FF), lambda i, be: (be[i], 0, 0)),
                  pl.BlockSpec((1, D_FF, D_MODEL), lambda i, be: (be[i], 0, 0)),
                  pl.BlockSpec((1, 1, D_MODEL), lambda i, be: (be[i], 0, 0))],
        out_specs=pl.BlockSpec((rb, D_MODEL), lambda i, be: (i, 0)),
    )
    return pl.pallas_call(
        _expert_body,
        grid_spec=grid_spec,
        out_shape=jax.ShapeDtypeStruct((nb * rb, D_MODEL), F32),
        compiler_params=_cparams(("arbitrary",)),
        name="experts",
    )(blk_e, xbuf, wgu, bgu.reshape(N_EXPERTS, 1, 2 * D_FF), wdn, bdn.reshape(N_EXPERTS, 1, D_MODEL))


def _moe(hf, x1, W):
    M = hf.shape[0]
    idx128, gate128 = _router(hf, W['router_w'], W['router_b'])
    top_i = idx128[:, :TOP_K]
    gates = gate128[:, :TOP_K]
    A = M * TOP_K
    rb = MOE_RB if A >= 32 * MOE_RB else 128
    e_flat = top_i.reshape(A)
    tok_flat = jnp.repeat(jnp.arange(M, dtype=I32), TOP_K)
    order = jnp.argsort(e_flat)
    e_s, tok_s = e_flat[order], tok_flat[order]
    counts = jnp.bincount(e_flat, length=N_EXPERTS)
    start = jnp.cumsum(counts) - counts
    padded = (counts + rb - 1) // rb * rb
    pend = jnp.cumsum(padded)
    pstart = pend - padded
    dest = (pstart[e_s] + (jnp.arange(A, dtype=I32) - start[e_s])).astype(I32)
    nb = (A + rb - 1) // rb + N_EXPERTS
    xbuf = jnp.zeros((nb * rb, D_MODEL), BF16).at[dest].set(hf[tok_s])
    blk_e = jnp.minimum(jnp.searchsorted(pend, jnp.arange(nb, dtype=I32) * rb, side='right'),
                        N_EXPERTS - 1).astype(I32)
    ybuf = _experts(xbuf, blk_e, W['exp_w_gu'], W['exp_b_gu'], W['exp_w_down'], W['exp_b_down'])
    dest_orig = jnp.zeros((A,), I32).at[order].set(dest).reshape(M, TOP_K)
    y = x1
    for k in range(TOP_K):
        y = y + ybuf[dest_orig[:, k]] * gates[:, k:k + 1]
    return y


def _rmsnorm_ref(x, g):
    xf = x.astype(F32)
    return xf * lax.rsqrt(jnp.mean(xf * xf, axis=-1, keepdims=True) + EPS) * g


def _sparse_attention_xla(q, k_sel, v_sel, idx, q_pos, rel_bias):
    B, Q = q.shape[:2]
    group = N_HEADS // N_KV_HEADS
    qg = q.reshape(B, Q, N_KV_HEADS, group, HEAD_DIM)
    logits = jnp.einsum('bqcgd,bqjcd->bqcgj', qg, k_sel).astype(F32) * HEAD_DIM ** -0.5
    dist = q_pos[None, :, None] - idx
    bias = rel_bias[_t5_bucket(dist)].astype(F32)
    bias = bias.reshape(B, Q, -1, N_KV_HEADS, group).transpose(0, 1, 3, 4, 2)
    valid = (dist >= 0)[:, :, None, None, :]
    p = jax.nn.softmax(jnp.where(valid, logits + bias, -jnp.inf), axis=-1)
    out = jnp.einsum('bqcgj,bqjcd->bqcgd', p.astype(v_sel.dtype), v_sel)
    return out.reshape(B, Q, N_HEADS * HEAD_DIM)


def _dsa_sample_xla(q, k_new, v_new, qi, wi, ki_new, cache_k, cache_v, cache_kidx, page_table, rel_bias):
    Bd, S = q.shape[:2]
    past = page_table.shape[1] * PAGE_SIZE
    topk = min(TOPK, (past + S) // 4)
    ki_past = cache_kidx[page_table].reshape(Bd, past, IDX_DIM)
    ki_all = jnp.concatenate([ki_past, ki_new], axis=1)
    q_pos = past + jnp.arange(S, dtype=I32)
    s = jnp.einsum('bqhd,bsd->bqhs', qi, ki_all).astype(F32) * IDX_DIM ** -0.5
    scores = jnp.einsum('bqhs,bqh->bqs', jax.nn.relu(s), wi.astype(F32)) * N_IDX_HEADS ** -0.5
    s_pos = jnp.arange(scores.shape[-1], dtype=I32)
    masked = jnp.where(s_pos[None, None, :] <= q_pos[None, :, None], scores, -jnp.inf)
    _, idx = lax.top_k(masked, topk)
    lp = jnp.minimum(idx, past - 1)
    phys = jnp.take_along_axis(page_table, (lp // PAGE_SIZE).reshape(Bd, -1), axis=1).reshape(idx.shape)
    off = lp % PAGE_SIZE
    new_i = jnp.clip(idx - past, 0, S - 1)
    is_past = (idx < past)[..., None, None]
    gather_rows = jax.vmap(lambda rows, ii: rows[ii])
    k_sel = jnp.where(is_past, cache_k[phys, off], gather_rows(k_new, new_i))
    v_sel = jnp.where(is_past, cache_v[phys, off], gather_rows(v_new, new_i))
    return _sparse_attention_xla(q, k_sel, v_sel, idx, q_pos, rel_bias)


def _mem_attention_xla(xq, mk, mv):
    B, T = xq.shape[:2]
    logits = jnp.einsum('bthd,bmhd->bhtm', xq, mk).astype(F32) * X_HEAD_DIM ** -0.5
    p = jax.nn.softmax(logits, axis=-1)
    out = jnp.einsum('bhtm,bmhd->bthd', p, mv)
    return out.reshape(B, T, X_HEADS * X_HEAD_DIM)


def _pad_heads_cols(w):
    K = w.shape[0]
    w = w.reshape(K, X_HEADS, X_HEAD_DIM)
    return jnp.pad(w, ((0, 0), (0, 0), (0, X_HEAD_PAD - X_HEAD_DIM))).reshape(K, X_HEADS * X_HEAD_PAD)


def _pad_heads_vec(g):
    return jnp.pad(g, (0, X_HEAD_PAD - X_HEAD_DIM))


def _prep_weights(P):
    W = {}
    w_in = P['w_in'].astype(BF16)
    widths = (D_RNN, D_RNN, N_HEADS * HEAD_DIM, N_KV_HEADS * HEAD_DIM, N_KV_HEADS * HEAD_DIM,
              N_IDX_HEADS * IDX_DIM, IDX_DIM, N_IDX_HEADS, X_HEADS * X_HEAD_DIM, 3 * D_MODEL)
    offs = np.cumsum((0,) + widths)
    seg = lambda i: w_in[:, offs[i]:offs[i + 1]]
    W['w_xy'] = w_in[:, offs[0]:offs[2]]
    W['w_q'], W['w_k'], W['w_v'], W['w_qi'] = seg(2), seg(3), seg(4), seg(5)
    W['w_kiwi'] = jnp.pad(w_in[:, offs[6]:offs[8]], ((0, 0), (0, 128 - IDX_DIM - N_IDX_HEADS)))
    W['w_xq'] = _pad_heads_cols(seg(8))
    W['w_g'] = seg(9)
    W['q_gain'] = jnp.tile(P['q_norm'], N_HEADS)
    W['k_gain'] = jnp.tile(P['k_norm'], N_KV_HEADS)
    W['xq_gain'] = jnp.tile(_pad_heads_vec(P['xq_norm']), X_HEADS)
    W['xk_gain'] = jnp.tile(_pad_heads_vec(P['xk_norm']), X_HEADS)
    wmk, wmv = jnp.split(P['w_mem_kv'].astype(BF16), 2, axis=-1)
    W['w_mem_k'] = _pad_heads_cols(wmk)
    W['w_mem_v'] = _pad_heads_cols(wmv)
    W['wg_lru'] = _block_diag_gates(P['lru_wa'], P['lru_wx'])
    W['w_br_rnn'] = P['w_br_rnn'].astype(BF16)
    W['w_br_attn'] = P['w_br_attn'].astype(BF16)
    W['w_br_mem'] = _pad_heads_cols(P['w_br_mem'].astype(BF16).T).T
    W['w_out'] = P['w_out'].astype(BF16)
    W['router_w'] = jnp.pad(P['router_w'].astype(BF16), ((0, 0), (0, 128 - N_EXPERTS)))
    W['router_b'] = jnp.pad(P['router_b'].astype(F32), (0, 128 - N_EXPERTS), constant_values=-1e30).reshape(1, 128)
    W['exp_w_gu'] = P['exp_w_gu'].astype(BF16)
    W['exp_w_down'] = P['exp_w_down'].astype(BF16)
    for name in ('norm_mix', 'conv_w', 'conv_b', 'lru_ba', 'lru_bx', 'lru_lambda', 'rel_bias', 'mem_norm',
                 'norm_ffn', 'exp_b_gu', 'exp_b_down'):
        W[name] = P[name]
    return W


def _unpad_heads(a):
    return a.reshape(a.shape[:-1] + (X_HEADS, X_HEAD_PAD))[..., :X_HEAD_DIM]


def _in_proj(x2, W):
    g = W['norm_mix']
    o = {}
    (o['xy'],) = _norm_proj(x2, g, W['w_xy'], [F32], tn=1024)
    (o['q'],) = _norm_proj(x2, g, W['w_q'], [BF16], tn=512, mode="headnorm", head_gain=W['q_gain'])
    o['k'], o['k16'] = _norm_proj(x2, g, W['w_k'], [F32, BF16], tn=512, mode="headnorm", head_gain=W['k_gain'])
    o['v'], o['v16'] = _norm_proj(x2, g, W['w_v'], [F32, BF16], tn=512)
    (o['qi'],) = _norm_proj(x2, g, W['w_qi'], [BF16], tn=512)
    (o['kiwi'],) = _norm_proj(x2, g, W['w_kiwi'], [F32], tn=128)
    (o['xq'],) = _norm_proj(x2, g, W['w_xq'], [BF16], tn=512, mode="headnorm", head_gain=W['xq_gain'],
                            hd_pad=X_HEAD_PAD, hd_true=X_HEAD_DIM)
    (o['g'],) = _norm_proj(x2, g, W['w_g'], [F32], tn=1024)
    return o


def kernel(x_prompt, x_sample, cache_k, cache_v, cache_kidx, cache_mem_k, cache_mem_v, state_conv, state_rglru, page_table, mem_prompt, norm_mix, w_in, conv_w, conv_b, lru_wa, lru_ba, lru_wx, lru_bx, lru_lambda, q_norm, k_norm, rel_bias, mem_norm, w_mem_kv, xq_norm, xk_norm, w_br_rnn, w_br_attn, w_br_mem, w_out, norm_ffn, router_w, router_b, exp_w_gu, exp_b_gu, exp_w_down, exp_b_down):
    W = _prep_weights(dict(norm_mix=norm_mix, w_in=w_in, conv_w=conv_w, conv_b=conv_b, lru_wa=lru_wa,
                           lru_ba=lru_ba, lru_wx=lru_wx, lru_bx=lru_bx, lru_lambda=lru_lambda, q_norm=q_norm,
                           k_norm=k_norm, rel_bias=rel_bias, mem_norm=mem_norm, w_mem_kv=w_mem_kv,
                           xq_norm=xq_norm, xk_norm=xk_norm, w_br_rnn=w_br_rnn, w_br_attn=w_br_attn,
                           w_br_mem=w_br_mem, w_out=w_out, norm_ffn=norm_ffn, router_w=router_w,
                           router_b=router_b, exp_w_gu=exp_w_gu, exp_b_gu=exp_b_gu, exp_w_down=exp_w_down,
                           exp_b_down=exp_b_down))
    lru = (W['conv_w'], W['conv_b'], W['wg_lru'], W['lru_ba'], W['lru_bx'], W['lru_lambda'])

    B, T, D = x_prompt.shape
    xp2 = x_prompt.reshape(B * T, D)
    memp2 = mem_prompt.reshape(B * N_MEM, D)
    mk_pad, mk16 = _norm_proj(memp2, W['mem_norm'], W['w_mem_k'], [F32, BF16], tn=512, mode="headnorm",
                              head_gain=W['xk_gain'], hd_pad=X_HEAD_PAD, hd_true=X_HEAD_DIM)
    mv_pad, mv16 = _norm_proj(memp2, W['mem_norm'], W['w_mem_v'], [F32, BF16], tn=512)
    mk_p = _unpad_heads(mk_pad).reshape(B, N_MEM, X_HEADS, X_HEAD_DIM)
    mv_p = _unpad_heads(mv_pad).reshape(B, N_MEM, X_HEADS, X_HEAD_DIM)

    o = _in_proj(xp2, W)
    xy = o['xy'].reshape(B, T, 2 * D_RNN)
    rnn_out, rg_p = _rglru_prompt(xy, *lru)
    conv_p = xy[:, T - (CONV_W - 1):, :D_RNN]
    ki_p = o['kiwi'][:, :IDX_DIM].reshape(B, T, IDX_DIM)
    wi_p = o['kiwi'][:, IDX_DIM:IDX_DIM + N_IDX_HEADS].reshape(B, T, N_IDX_HEADS)
    qi_h = o['qi'].reshape(B, T, N_IDX_HEADS, IDX_DIM).transpose(0, 2, 1, 3)
    kit = ki_p.astype(BF16).transpose(0, 2, 1)
    attn_out = _dsa_prompt(o['q'].reshape(B, T, -1), o['k16'].reshape(B, T, -1), o['v16'].reshape(B, T, -1),
                           qi_h, wi_p, kit, W['rel_bias'])
    mem_out = _mem_attn(o['xq'].reshape(B, T, -1), mk16.reshape(B, N_MEM, -1), mv16.reshape(B, N_MEM, -1))
    x1, hf = _merge(xp2, rnn_out.reshape(B * T, -1), attn_out.reshape(B * T, -1), mem_out.reshape(B * T, -1),
                    o['g'], W['w_br_rnn'], W['w_br_attn'], W['w_br_mem'], W['w_out'], W['norm_ffn'])
    y_prompt = _moe(hf, x1, W).reshape(B, T, D)
    k_p = o['k'].reshape(B, T, N_KV_HEADS, HEAD_DIM)
    v_p = o['v'].reshape(B, T, N_KV_HEADS, HEAD_DIM)

    Bd, S, _ = x_sample.shape
    xs2 = x_sample.reshape(Bd * S, D)
    os_ = _in_proj(xs2, W)
    rnn_s, rg_s = _rglru_step(os_['xy'], state_conv.transpose(1, 0, 2), state_rglru, *lru)
    conv_s = jnp.concatenate([state_conv[:, 1:], os_['xy'][:, None, :D_RNN]], axis=1)
    k_s = os_['k'].reshape(Bd, S, N_KV_HEADS, HEAD_DIM)
    v_s = os_['v'].reshape(Bd, S, N_KV_HEADS, HEAD_DIM)
    ki_s = os_['kiwi'][:, :IDX_DIM].reshape(Bd, S, IDX_DIM)
    wi_s = os_['kiwi'][:, IDX_DIM:IDX_DIM + N_IDX_HEADS].reshape(Bd, S, N_IDX_HEADS)
    attn_s = _dsa_sample_xla(os_['q'].astype(F32).reshape(Bd, S, N_HEADS, HEAD_DIM), k_s, v_s,
                             os_['qi'].astype(F32).reshape(Bd, S, N_IDX_HEADS, IDX_DIM), wi_s, ki_s,
                             cache_k, cache_v, cache_kidx, page_table, W['rel_bias'])
    xq_s = _unpad_heads(os_['xq'].astype(F32)).reshape(Bd, S, X_HEADS, X_HEAD_DIM)
    mem_s = _mem_attention_xla(xq_s, cache_mem_k, cache_mem_v)
    mem_s = jnp.pad(mem_s.reshape(Bd * S, X_HEADS, X_HEAD_DIM),
                    ((0, 0), (0, 0), (0, X_HEAD_PAD - X_HEAD_DIM))).reshape(Bd * S, -1)
    x1s, hfs = _merge(xs2, rnn_s, attn_s.reshape(Bd * S, -1).astype(BF16), mem_s.astype(BF16), os_['g'],
                      W['w_br_rnn'], W['w_br_attn'], W['w_br_mem'], W['w_out'], W['norm_ffn'])
    y_sample = _moe(hfs, x1s, W).reshape(Bd, S, D)

    return (y_prompt, y_sample, k_p, v_p, ki_p, mk_p, mv_p, conv_p, rg_p.reshape(B, D_RNN),
            k_s, v_s, ki_s, conv_s, rg_s)
```

```python
import functools
import math

import numpy as np
import jax
import jax.numpy as jnp
from jax import lax
from jax.experimental import pallas as pl
from jax.experimental.pallas import tpu as pltpu

F32 = jnp.float32
BF16 = jnp.bfloat16
I32 = jnp.int32

D_MODEL = 1024
D_RNN = 1024
LRU_BLOCKS = 16
LRU_BW = 64
CONV_W = 4
LRU_C = 8.0
N_HEADS = 8
N_KV_HEADS = 4
HEAD_DIM = 128
N_IDX_HEADS = 8
IDX_DIM = 64
TOPK = 256
PAGE_SIZE = 128
N_MEM = 256
X_HEADS = 4
X_HEAD_DIM = 192
X_HEAD_PAD = 256
N_BUCKETS = 32
MAX_DISTANCE = 128
N_EXPERTS = 32
TOP_K = 4
D_FF = 1024
SWIGLU_LIMIT = 7.0
SWIGLU_ALPHA = 1.702
EPS = 1e-6

INT_MIN = -2 ** 31
NEG = -1e30
LOG2E = 1.4426950408889634
VMEM_LIMIT = 56 * 1024 * 1024

DSA_TQ = 256
DSA_KB = 1024
MOE_RB = 256
MOE_TM = 256
SEG_ALIGN = 8


def _cparams(sem):
    return pltpu.CompilerParams(dimension_semantics=sem, vmem_limit_bytes=VMEM_LIMIT)


def _sigmoid(x):
    return 1.0 / (1.0 + jnp.exp(-x))


def _gelu_tanh(x):
    return 0.5 * x * (1.0 + jnp.tanh(0.7978845608028654 * (x + 0.044715 * x * x * x)))


def _head_norm(y, gain, hd_pad, hd_true):
    parts = []
    for s in range(0, y.shape[1], hd_pad):
        ys = y[:, s:s + hd_pad]
        ms = jnp.sum(ys * ys, axis=-1, keepdims=True) * (1.0 / hd_true)
        parts.append(ys * lax.rsqrt(ms + EPS) * gain[:, s:s + hd_pad])
    return parts[0] if len(parts) == 1 else jnp.concatenate(parts, axis=-1)


def _proj_body(x_ref, g_ref, w_ref, *refs, mode, hd_pad, hd_true, n_out):
    if mode == "headnorm":
        gain_ref, refs = refs[0], refs[1:]
    outs, h_ref = refs[:n_out], refs[n_out]

    @pl.when(pl.program_id(1) == 0)
    def _():
        x = x_ref[...]
        ms = jnp.mean(x * x, axis=-1, keepdims=True)
        h_ref[...] = (x * lax.rsqrt(ms + EPS) * g_ref[...]).astype(BF16)

    y = jnp.dot(h_ref[...], w_ref[...], preferred_element_type=F32)
    if mode == "headnorm":
        y = _head_norm(y, gain_ref[...], hd_pad, hd_true)
    for o in outs:
        o[...] = y.astype(o.dtype)


def _norm_proj(x, gain, w, out_dtypes, *, tn, mode="plain", head_gain=None, hd_pad=128, hd_true=128):
    M, K = x.shape
    N = w.shape[1]
    tm = min(512, M)
    assert M % tm == 0 and N % tn == 0
    in_specs = [pl.BlockSpec((tm, K), lambda i, j: (i, 0)),
                pl.BlockSpec((1, K), lambda i, j: (0, 0)),
                pl.BlockSpec((K, tn), lambda i, j: (0, j))]
    args = [x, gain.reshape(1, K), w]
    if mode == "headnorm":
        in_specs.append(pl.BlockSpec((1, tn), lambda i, j: (0, j)))
        args.append(head_gain.reshape(1, N))
    body = functools.partial(_proj_body, mode=mode, hd_pad=hd_pad, hd_true=hd_true, n_out=len(out_dtypes))
    return pl.pallas_call(
        body,
        grid=(M // tm, N // tn),
        in_specs=in_specs,
        out_specs=[pl.BlockSpec((tm, tn), lambda i, j: (i, j)) for _ in out_dtypes],
        out_shape=[jax.ShapeDtypeStruct((M, N), dt) for dt in out_dtypes],
        scratch_shapes=[pltpu.VMEM((tm, K), BF16)],
        compiler_params=_cparams(("parallel", "arbitrary")),
        name="norm_proj_" + mode,
    )(*args)


def _softplus(z):
    return jnp.maximum(z, 0.0) + jnp.log1p(jnp.exp(-jnp.abs(z)))


def _lru_terms(xc, wg_ref, ba, bx, lam, a_out, b_out):
    sp = _softplus(-lam)
    for gi in range(4):
        sl = slice(gi * 256, (gi + 1) * 256)
        xg = xc[:, sl]
        z = jnp.dot(xg.astype(BF16), wg_ref[gi], preferred_element_type=F32)
        r = _sigmoid(z[:, :256] + ba[:, sl])
        ig = _sigmoid(z[:, 256:] + bx[:, sl])
        log_a = -LRU_C * r * sp[:, sl]
        a_out[:, sl] = jnp.exp(log_a)
        b_out[:, sl] = jnp.sqrt(1.0 - jnp.exp(2.0 * log_a)) * ig * xg


def _rglru_prompt_body(xr_ref, yr_ref, cw_ref, cb_ref, wg_ref, ba_ref, bx_ref, lam_ref,
                       out_ref, hl_ref, xp_ref, a_ref, b_ref, h_ref, hc_ref, *, tc):
    t = pl.program_id(1)

    @pl.when(t == 0)
    def _():
        xp_ref[0:8, :] = jnp.zeros((8, D_RNN), F32)
        hc_ref[...] = jnp.zeros_like(hc_ref)

    @pl.when(t > 0)
    def _():
        xp_ref[0:8, :] = xp_ref[tc:tc + 8, :]

    xp_ref[8:8 + tc, :] = xr_ref[0]
    cw = cw_ref[...]
    xc = cb_ref[...] + cw[0:1] * xp_ref[5:5 + tc, :]
    xc = xc + cw[1:2] * xp_ref[6:6 + tc, :]
    xc = xc + cw[2:3] * xp_ref[7:7 + tc, :]
    xc = xc + cw[3:4] * xp_ref[8:8 + tc, :]
    _lru_terms(xc, wg_ref, ba_ref[...], bx_ref[...], lam_ref[...], a_ref, b_ref)

    def step(i, h):
        h = a_ref[pl.ds(i, 1), :] * h + b_ref[pl.ds(i, 1), :]
        h_ref[pl.ds(i, 1), :] = h
        return h

    h = lax.fori_loop(0, tc, step, hc_ref[0:1, :], unroll=8)
    hc_ref[0:1, :] = h
    out_ref[0] = (h_ref[...] * _gelu_tanh(yr_ref[0])).astype(out_ref.dtype)

    @pl.when(t == pl.num_programs(1) - 1)
    def _():
        hl_ref[0] = h


def _rglru_prompt(xy, conv_w, conv_b, wg, ba, bx, lam):
    B, T, _ = xy.shape
    tc = min(256, T)
    vec = lambda a: a.reshape(1, D_RNN)
    full = lambda shape: pl.BlockSpec(shape, lambda b, t: (0,) * len(shape))
    return pl.pallas_call(
        functools.partial(_rglru_prompt_body, tc=tc),
        grid=(B, T // tc),
        in_specs=[pl.BlockSpec((1, tc, D_RNN), lambda b, t: (b, t, 0)),
                  pl.BlockSpec((1, tc, D_RNN), lambda b, t: (b, t, 1)),
                  full((CONV_W, D_RNN)), full((1, D_RNN)), full((4, 256, 512)),
                  full((1, D_RNN)), full((1, D_RNN)), full((1, D_RNN))],
        out_specs=[pl.BlockSpec((1, tc, D_RNN), lambda b, t: (b, t, 0)),
                   pl.BlockSpec((1, 1, D_RNN), lambda b, t: (b, 0, 0))],
        out_shape=[jax.ShapeDtypeStruct((B, T, D_RNN), BF16),
                   jax.ShapeDtypeStruct((B, 1, D_RNN), F32)],
        scratch_shapes=[pltpu.VMEM((tc + 8, D_RNN), F32), pltpu.VMEM((tc, D_RNN), F32),
                        pltpu.VMEM((tc, D_RNN), F32), pltpu.VMEM((tc, D_RNN), F32),
                        pltpu.VMEM((8, D_RNN), F32)],
        compiler_params=_cparams(("parallel", "arbitrary")),
        name="rglru_prompt",
    )(xy, xy, conv_w, vec(conv_b), wg, vec(ba), vec(bx), vec(lam))


def _rglru_step_body(xr_ref, yr_ref, prev_ref, h0_ref, cw_ref, cb_ref, wg_ref, ba_ref, bx_ref, lam_ref,
                     out_ref, hn_ref, a_ref, b_ref):
    cw = cw_ref[...]
    xc = cb_ref[...] + cw[0:1] * prev_ref[0] + cw[1:2] * prev_ref[1] + cw[2:3] * prev_ref[2] + cw[3:4] * xr_ref[...]
    _lru_terms(xc, wg_ref, ba_ref[...], bx_ref[...], lam_ref[...], a_ref, b_ref)
    h = a_ref[...] * h0_ref[...] + b_ref[...]
    hn_ref[...] = h
    out_ref[...] = (h * _gelu_tanh(yr_ref[...])).astype(out_ref.dtype)


def _rglru_step(xy, prev, h0, conv_w, conv_b, wg, ba, bx, lam):
    R = xy.shape[0]
    vec = lambda a: a.reshape(1, D_RNN)
    full = lambda shape: pl.BlockSpec(shape, lambda i: (0,) * len(shape))
    return pl.pallas_call(
        _rglru_step_body,
        grid=(1,),
        in_specs=[pl.BlockSpec((R, D_RNN), lambda i: (0, 0)), pl.BlockSpec((R, D_RNN), lambda i: (0, 1)),
                  full((CONV_W - 1, R, D_RNN)), full((R, D_RNN)),
                  full((CONV_W, D_RNN)), full((1, D_RNN)), full((4, 256, 512)),
                  full((1, D_RNN)), full((1, D_RNN)), full((1, D_RNN))],
        out_specs=[full((R, D_RNN)), full((R, D_RNN))],
        out_shape=[jax.ShapeDtypeStruct((R, D_RNN), BF16), jax.ShapeDtypeStruct((R, D_RNN), F32)],
        scratch_shapes=[pltpu.VMEM((R, D_RNN), F32), pltpu.VMEM((R, D_RNN), F32)],
        compiler_params=_cparams(("arbitrary",)),
        name="rglru_step",
    )(xy, xy, prev, h0, conv_w, vec(conv_b), wg, vec(ba), vec(bx), vec(lam))


def _block_diag_gates(wa, wx):
    def bd(w):
        w4 = w.reshape(4, 4, LRU_BW, LRU_BW)
        return jnp.einsum('gaij,ab->gaibj', w4, jnp.eye(4, dtype=w.dtype)).reshape(4, 256, 256)
    return jnp.concatenate([bd(wa), bd(wx)], axis=-1).astype(BF16)


def _sortable_key(s):
    bits = pltpu.bitcast(s + 0.0, I32)
    return jnp.where(bits >= 0, bits, bits ^ jnp.int32(0x7FFFFFFF))


def _topk_threshold(count_ge, rows):
    cnt0 = count_ge(jnp.zeros((rows, 1), I32))
    pos = cnt0 >= TOPK
    base = jnp.where(pos, jnp.int32(0), jnp.int32(INT_MIN))
    cnt = jnp.where(pos, cnt0, jnp.int32(TOPK))

    def bit_body(i, carry):
        base, cnt = carry
        cand = base | jnp.left_shift(jnp.int32(1), jnp.int32(30) - i)
        c = count_ge(cand)
        ok = c >= TOPK
        return jnp.where(ok, cand, base), jnp.where(ok, c, cnt)

    return lax.fori_loop(0, 31, bit_body, (base, cnt))


def _attend_block(q_ref, k, v, mask, bias_of_head, m_ref, l_ref, acc_ref):
    for h in range(N_HEADS):
        c = h // (N_HEADS // N_KV_HEADS)
        hs = slice(h * HEAD_DIM, (h + 1) * HEAD_DIM)
        cs = slice(c * HEAD_DIM, (c + 1) * HEAD_DIM)
        s = lax.dot_general(q_ref[0, :, hs], k[:, cs], (((1,), (1,)), ((), ())),
                            preferred_element_type=F32)
        bias = bias_of_head(h)
        s = jnp.where(mask, s if bias is None else s + bias, NEG)
        m_old = m_ref[h]
        m_new = jnp.maximum(m_old, jnp.max(s, axis=-1, keepdims=True))
        alpha = jnp.exp2(m_old - m_new)
        p = jnp.exp2(s - m_new)
        l_ref[h] = alpha * l_ref[h] + jnp.sum(p, axis=-1, keepdims=True)
        acc_ref[:, hs] = alpha * acc_ref[:, hs] + jnp.dot(p.astype(BF16), v[:, cs],
                                                          preferred_element_type=F32)
        m_ref[h] = m_new


def _dsa_prompt_body(qi_ref, wi_ref, kit_ref, q_ref, kf_ref, vf_ref, kp_ref, vp_ref,
                     kd_ref, vd_ref, tbd_ref, tbp_ref, out_ref,
                     s_ref, thr_ref, m_ref, l_ref, acc_ref, *, tq, kb):
    qb = pl.program_id(1)
    st = pl.program_id(2)
    t0 = qb * tq
    far_len = jnp.maximum(qb - 1, 0) * tq
    n_far = (far_len + kb - 1) // kb

    @pl.when(st == 0)
    def _():
        m_ref[...] = jnp.full(m_ref.shape, NEG, F32)
        l_ref[...] = jnp.zeros_like(l_ref)
        acc_ref[...] = jnp.zeros_like(acc_ref)

        nkc = (t0 + tq + kb - 1) // kb
        wi = wi_ref[0]
        row = t0 + lax.broadcasted_iota(I32, (tq, kb), 0)
        lane = lax.broadcasted_iota(I32, (tq, kb), 1)

        def score_chunk(c, carry):
            off = pl.multiple_of(c * kb, kb)
            kic = kit_ref[0, :, pl.ds(off, kb)]
            acc = jnp.zeros((tq, kb), F32)
            for h in range(N_IDX_HEADS):
                s = jnp.dot(qi_ref[0, h], kic, preferred_element_type=F32)
                acc = acc + jnp.maximum(s, 0.0) * wi[:, h:h + 1]
            key = jnp.where(off + lane <= row, _sortable_key(acc), jnp.int32(INT_MIN))
            s_ref[:, pl.ds(off, kb)] = key
            return carry

        lax.fori_loop(0, nkc, score_chunk, 0)

        def count_ge(cand):
            def body(c, part):
                off = pl.multiple_of(c * kb, kb)
                ge = (s_ref[:, pl.ds(off, kb)] >= cand).astype(I32)
                for j in range(kb // 128):
                    part = part + ge[:, j * 128:(j + 1) * 128]
                return part
            part = lax.fori_loop(0, nkc, body, jnp.zeros((tq, 128), I32))
            return jnp.sum(part, axis=-1, keepdims=True)

        base, cnt_base = _topk_threshold(count_ge, tq)
        thr = jnp.maximum(base, jnp.int32(INT_MIN + 1))
        thr_ref[...] = thr

        tie = jnp.logical_and(base > INT_MIN, cnt_base > TOPK)

        @pl.when(jnp.max(tie.astype(I32)) > 0)
        def _():
            n_gt = count_ge(base + 1)
            need = jnp.where(tie, TOPK - n_gt, jnp.int32(2 ** 30)).astype(F32)
            sub = 256
            incl = (lax.broadcasted_iota(I32, (sub, sub), 0) <= lax.broadcasted_iota(I32, (sub, sub), 1)).astype(BF16)

            def tie_chunk(c, seen):
                for j in range(kb // sub):
                    off = pl.multiple_of(c * kb + j * sub, sub)
                    blk = s_ref[:, pl.ds(off, sub)]
                    eq = (blk == base).astype(F32)
                    rank = jnp.dot(eq.astype(BF16), incl, preferred_element_type=F32) + seen
                    demote = eq * (rank - need) > 0.0
                    s_ref[:, pl.ds(off, sub)] = jnp.where(demote, base - 1, blk)
                    seen = seen + jnp.sum(eq, axis=-1, keepdims=True)
                return seen

            lax.fori_loop(0, nkc, tie_chunk, jnp.zeros((tq, 1), F32))

        @pl.when(qb > 0)
        def _():
            off = pl.multiple_of(t0 - tq, tq)
            mask = s_ref[:, pl.ds(off, tq)] >= thr
            _attend_block(q_ref, kp_ref[0], vp_ref[0], mask, lambda h: tbp_ref[h], m_ref, l_ref, acc_ref)

        off = pl.multiple_of(t0, tq)
        mask = s_ref[:, pl.ds(off, tq)] >= thr
        _attend_block(q_ref, kd_ref[0], vd_ref[0], mask, lambda h: tbd_ref[h], m_ref, l_ref, acc_ref)

    @pl.when(jnp.logical_and(st > 0, st - 1 < n_far))
    def _():
        off = pl.multiple_of((st - 1) * kb, kb)
        col = off + lax.broadcasted_iota(I32, (tq, kb), 1)
        keys = jnp.where(col < far_len, s_ref[:, pl.ds(off, kb)], jnp.int32(INT_MIN))
        _attend_block(q_ref, kf_ref[0], vf_ref[0], keys >= thr_ref[...], lambda h: None, m_ref, l_ref, acc_ref)

    @pl.when(st == pl.num_programs(2) - 1)
    def _():
        for h in range(N_HEADS):
            hs = slice(h * HEAD_DIM, (h + 1) * HEAD_DIM)
            out_ref[0, :, hs] = (acc_ref[:, hs] / l_ref[h]).astype(out_ref.dtype)


def _t5_bucket(dist):
    n = jnp.maximum(dist, 0)
    max_exact = N_BUCKETS // 2
    nf = jnp.maximum(n, 1).astype(F32)
    large = max_exact + (jnp.log(nf / max_exact) / math.log(MAX_DISTANCE / max_exact)
                         * (N_BUCKETS - max_exact)).astype(I32)
    large = jnp.minimum(large, N_BUCKETS - 1)
    return jnp.where(n < max_exact, n, large)


_FAR_DIST = int(math.ceil((N_BUCKETS // 2) * (MAX_DISTANCE / (N_BUCKETS // 2)) **
                          ((N_BUCKETS - 1 - N_BUCKETS // 2) / (N_BUCKETS - N_BUCKETS // 2)))) + 1


def _dsa_prompt(q, k, v, qi_h, wi, kit, rel_bias):
    B, T, _ = q.shape
    tq = min(DSA_TQ, T)
    kb = min(DSA_KB, T)
    assert T % tq == 0 and T % kb == 0 and kb % tq == 0 and tq >= _FAR_DIST and T >= 4 * TOPK
    nq = T // tq
    n_far_max = max(1, ((nq - 2) * tq + kb - 1) // kb) if nq > 1 else 1
    d = jnp.arange(tq, dtype=I32)[:, None] - jnp.arange(tq, dtype=I32)[None, :]
    rel = (rel_bias - rel_bias[N_BUCKETS - 1]).astype(F32) * LOG2E
    tb_diag = rel[_t5_bucket(d)].transpose(2, 0, 1)
    tb_prev = rel[_t5_bucket(d + tq)].transpose(2, 0, 1)

    def far_idx(b, qb, st):
        far_len = jnp.maximum(qb - 1, 0) * tq
        n_far = (far_len + kb - 1) // kb
        return (b, jnp.clip(st - 1, 0, jnp.maximum(n_far - 1, 0)), 0)

    kv_far = pl.BlockSpec((1, kb, N_KV_HEADS * HEAD_DIM), far_idx)
    kv_prev = pl.BlockSpec((1, tq, N_KV_HEADS * HEAD_DIM), lambda b, qb, st: (b, jnp.maximum(qb - 1, 0), 0))
    kv_diag = pl.BlockSpec((1, tq, N_KV_HEADS * HEAD_DIM), lambda b, qb, st: (b, qb, 0))
    tile = pl.BlockSpec((N_HEADS, tq, tq), lambda b, qb, st: (0, 0, 0))
    return pl.pallas_call(
        functools.partial(_dsa_prompt_body, tq=tq, kb=kb),
        grid=(B, nq, n_far_max + 1),
        in_specs=[pl.BlockSpec((1, N_IDX_HEADS, tq, IDX_DIM), lambda b, qb, st: (b, 0, qb, 0)),
                  pl.BlockSpec((1, tq, N_IDX_HEADS), lambda b, qb, st: (b, qb, 0)),
                  pl.BlockSpec((1, IDX_DIM, T), lambda b, qb, st: (b, 0, 0)),
                  pl.BlockSpec((1, tq, N_HEADS * HEAD_DIM), lambda b, qb, st: (b, qb, 0)),
                  kv_far, kv_far, kv_prev, kv_prev, kv_diag, kv_diag, tile, tile],
        out_specs=pl.BlockSpec((1, tq, N_HEADS * HEAD_DIM), lambda b, qb, st: (b, qb, 0)),
        out_shape=jax.ShapeDtypeStruct((B, T, N_HEADS * HEAD_DIM), BF16),
        scratch_shapes=[pltpu.VMEM((tq, T), I32), pltpu.VMEM((tq, 1), I32),
                        pltpu.VMEM((N_HEADS, tq, 1), F32), pltpu.VMEM((N_HEADS, tq, 1), F32),
                        pltpu.VMEM((tq, N_HEADS * HEAD_DIM), F32)],
        compiler_params=_cparams(("parallel", "arbitrary", "arbitrary")),
        name="dsa_prompt",
    )(qi_h, wi, kit, q, k, v, k, v, k, v, tb_diag, tb_prev)


def _mem_attn_body(xq_ref, mk_ref, mv_ref, out_ref):
    scale = X_HEAD_DIM ** -0.5
    for h in range(X_HEADS):
        hs = slice(h * X_HEAD_PAD, (h + 1) * X_HEAD_PAD)
        s = lax.dot_general(xq_ref[0, :, hs], mk_ref[0, :, hs], (((1,), (1,)), ((), ())),
                            preferred_element_type=F32) * scale
        m = jnp.max(s, axis=-1, keepdims=True)
        p = jnp.exp(s - m)
        p = p / jnp.sum(p, axis=-1, keepdims=True)
        out_ref[0, :, hs] = jnp.dot(p.astype(BF16), mv_ref[0, :, hs],
                                    preferred_element_type=F32).astype(out_ref.dtype)


def _mem_attn(xq, mk, mv):
    B, T, W = xq.shape
    M = mk.shape[1]
    tm = min(512, T)
    return pl.pallas_call(
        _mem_attn_body,
        grid=(B, T // tm),
        in_specs=[pl.BlockSpec((1, tm, W), lambda b, i: (b, i, 0)),
                  pl.BlockSpec((1, M, W), lambda b, i: (b, 0, 0)),
                  pl.BlockSpec((1, M, W), lambda b, i: (b, 0, 0))],
        out_specs=pl.BlockSpec((1, tm, W), lambda b, i: (b, i, 0)),
        out_shape=jax.ShapeDtypeStruct((B, T, W), BF16),
        compiler_params=_cparams(("parallel", "parallel")),
        name="mem_attn",
    )(xq, mk, mv)


def _merge_body(x_ref, rnn_ref, att_ref, mem_ref, g_ref, wr_ref, wa_ref, wm_ref, wo_ref, gn_ref,
                x1_ref, hf_ref):
    mixed = _sigmoid(g_ref[:, 0:D_MODEL]) * jnp.dot(rnn_ref[...], wr_ref[...], preferred_element_type=F32)
    mixed = mixed + _sigmoid(g_ref[:, D_MODEL:2 * D_MODEL]) * jnp.dot(att_ref[...], wa_ref[...],
                                                                     preferred_element_type=F32)
    mixed = mixed + _sigmoid(g_ref[:, 2 * D_MODEL:3 * D_MODEL]) * jnp.dot(mem_ref[...], wm_ref[...],
                                                                         preferred_element_type=F32)
    x1 = x_ref[...] + jnp.dot(mixed.astype(BF16), wo_ref[...], preferred_element_type=F32)
    x1_ref[...] = x1
    ms = jnp.mean(x1 * x1, axis=-1, keepdims=True)
    hf_ref[...] = (x1 * lax.rsqrt(ms + EPS) * gn_ref[...]).astype(BF16)


def _merge(x, rnn, att, mem, g, wr, wa, wm, wo, gain):
    M = x.shape[0]
    tm = min(256, M)
    row = lambda w: pl.BlockSpec((tm, w), lambda i: (i, 0))
    full = lambda a: pl.BlockSpec(a.shape, lambda i: (0,) * a.ndim)
    gain = gain.reshape(1, D_MODEL)
    return pl.pallas_call(
        _merge_body,
        grid=(M // tm,),
        in_specs=[row(D_MODEL), row(rnn.shape[1]), row(att.shape[1]), row(mem.shape[1]), row(3 * D_MODEL),
                  full(wr), full(wa), full(wm), full(wo), full(gain)],
        out_specs=[row(D_MODEL), row(D_MODEL)],
        out_shape=[jax.ShapeDtypeStruct((M, D_MODEL), F32), jax.ShapeDtypeStruct((M, D_MODEL), BF16)],
        compiler_params=_cparams(("parallel",)),
        name="merge",
    )(x, rnn, att, mem, g, wr, wa, wm, wo, gain)


def _router_body(h_ref, w_ref, b_ref, pos_ref, gate_ref, cnt_ref, *, tm):
    logits = jnp.dot(h_ref[...], w_ref[...], preferred_element_type=F32) + b_ref[...]
    lane = lax.broadcasted_iota(I32, logits.shape, 1)
    vals, idxs = [], []
    for _ in range(TOP_K):
        m = jnp.max(logits, axis=-1, keepdims=True)
        i = jnp.min(jnp.where(logits == m, lane, jnp.int32(128)), axis=-1, keepdims=True)
        vals.append(m)
        idxs.append(i)
        logits = jnp.where(lane == i, -jnp.inf, logits)
    es = [jnp.exp(v - vals[0]) for v in vals]
    tot = es[0] + es[1] + es[2] + es[3]

    hots = [lane == i for i in idxs]
    hotf = [h.astype(F32) for h in hots]
    cnts = [jnp.sum(h, axis=0, keepdims=True) for h in hotf]
    c_i = (cnts[0] + cnts[1] + cnts[2] + cnts[3]).astype(I32)
    cpad = (((c_i + (SEG_ALIGN - 1)) // SEG_ALIGN) * SEG_ALIGN).astype(F32)
    r128 = lax.broadcasted_iota(I32, (128, 128), 0)
    c128 = lax.broadcasted_iota(I32, (128, 128), 1)
    loff = jnp.dot(jnp.broadcast_to(cpad, (8, 128)).astype(BF16), (r128 < c128).astype(BF16),
                   preferred_element_type=F32)[0:1]
    low = (lax.broadcasted_iota(I32, (tm, tm), 0) > lax.broadcasted_iota(I32, (tm, tm), 1)).astype(BF16)
    before = loff
    po = jnp.zeros(lane.shape, I32)
    go = jnp.zeros(lane.shape, F32)
    for k in range(TOP_K):
        pref = jnp.dot(low, hotf[k].astype(BF16), preferred_element_type=F32)
        pos = jnp.sum(jnp.where(hots[k], before + pref, 0.0), axis=-1, keepdims=True)
        before = before + cnts[k]
        po = jnp.where(lane == k, pos.astype(I32), po)
        go = jnp.where(lane == k, es[k] / tot, go)
    pos_ref[...] = po
    gate_ref[...] = go
    cnt_ref[0] = jnp.broadcast_to(c_i, (8, 128))


def _router(hf, rw, rb, tm):
    M = hf.shape[0]
    nt = M // tm
    return pl.pallas_call(
        functools.partial(_router_body, tm=tm),
        grid=(nt,),
        in_specs=[pl.BlockSpec((tm, D_MODEL), lambda i: (i, 0)),
                  pl.BlockSpec((D_MODEL, 128), lambda i: (0, 0)),
                  pl.BlockSpec((1, 128), lambda i: (0, 0))],
        out_specs=[pl.BlockSpec((tm, 128), lambda i: (i, 0)), pl.BlockSpec((tm, 128), lambda i: (i, 0)),
                   pl.BlockSpec((1, 8, 128), lambda i: (i, 0, 0))],
        out_shape=[jax.ShapeDtypeStruct((M, 128), I32), jax.ShapeDtypeStruct((M, 128), F32),
                   jax.ShapeDtypeStruct((nt, 8, 128), I32)],
        compiler_params=_cparams(("parallel",)),
        name="router",
    )(hf, rw, rb)


def _segment_copies(tile, n8_ref, loff_ref, gdst_ref, vm_ref, hbm_ref, sem, nbits, to_hbm, start):
    def body(e, carry):
        j = tile * N_EXPERTS + e
        n8, lo, gd = n8_ref[j], loff_ref[j], gdst_ref[j]
        for b in range(nbits):
            size = SEG_ALIGN << b
            off = (n8 & ((1 << b) - 1)) * SEG_ALIGN

            @pl.when(((n8 >> b) & 1) == 1)
            def _():
                v = vm_ref.at[pl.ds(pl.multiple_of(lo + off, SEG_ALIGN), size)]
                h = hbm_ref.at[pl.ds(pl.multiple_of(gd + off, SEG_ALIGN), size)]
                cp = pltpu.make_async_copy(v, h, sem) if to_hbm else pltpu.make_async_copy(h, v, sem)
                if start:
                    cp.start()
                else:
                    cp.wait()
        return carry

    lax.fori_loop(0, N_EXPERTS, body, 0)


def _dispatch_body(n8_ref, loff_ref, gdst_ref, pos_ref, h_ref, xinit_ref, xbuf_ref, xs_ref, sem, *, tm, rows, nbits):
    del xinit_ref
    tile = pl.program_id(0)
    pos_t = pos_ref[...].astype(F32).T
    sub = lax.broadcasted_iota(I32, (rows, tm), 0).astype(F32)
    perm = jnp.zeros((rows, tm), F32)
    for k in range(TOP_K):
        perm = perm + jnp.where(sub == pos_t[k:k + 1, :], 1.0, 0.0)
    xs_ref[...] = jnp.dot(perm.astype(BF16), h_ref[...], preferred_element_type=F32)
    _segment_copies(tile, n8_ref, loff_ref, gdst_ref, xs_ref, xbuf_ref, sem, nbits, True, True)
    _segment_copies(tile, n8_ref, loff_ref, gdst_ref, xs_ref, xbuf_ref, sem, nbits, True, False)


def _combine_body(n8_ref, loff_ref, gdst_ref, pos_ref, gate_ref, x1_ref, ybuf_ref, out_ref, ys_ref, sem,
                  *, tm, rows, nbits):
    tile = pl.program_id(0)

    @pl.when(tile == 0)
    def _():
        ys_ref[...] = jnp.zeros_like(ys_ref)

    _segment_copies(tile, n8_ref, loff_ref, gdst_ref, ys_ref, ybuf_ref, sem, nbits, False, True)
    lane = lax.broadcasted_iota(I32, (tm, rows), 1)
    pos = pos_ref[...]
    gate = gate_ref[...]
    g = jnp.zeros((tm, rows), F32)
    for k in range(TOP_K):
        g = g + jnp.where(lane == pos[:, k:k + 1], gate[:, k:k + 1], 0.0)
    g_hi = g.astype(BF16)
    g_lo = (g - g_hi.astype(F32)).astype(BF16)
    _segment_copies(tile, n8_ref, loff_ref, gdst_ref, ys_ref, ybuf_ref, sem, nbits, False, False)
    ys = ys_ref[...]
    y_hi = ys.astype(BF16)
    y_lo = (ys - y_hi.astype(F32)).astype(BF16)
    acc = jnp.dot(g_hi, y_hi, preferred_element_type=F32)
    acc = acc + jnp.dot(g_hi, y_lo, preferred_element_type=F32)
    acc = acc + jnp.dot(g_lo, y_hi, preferred_element_type=F32)
    out_ref[...] = x1_ref[...] + acc


def _expert_body(be_ref, nu_ref, x_ref, wgu_ref, bgu_ref, wdn_ref, bdn_ref, y_ref):
    @pl.when(pl.program_id(0) < nu_ref[0])
    def _():
        hgu = jnp.dot(x_ref[...].astype(BF16), wgu_ref[0], preferred_element_type=F32) + bgu_ref[0]
        gate = jnp.minimum(hgu[:, :D_FF], SWIGLU_LIMIT)
        up = jnp.clip(hgu[:, D_FF:], -SWIGLU_LIMIT, SWIGLU_LIMIT)
        act = (up + 1.0) * gate * _sigmoid(SWIGLU_ALPHA * gate)
        y_ref[...] = jnp.dot(act.astype(BF16), wdn_ref[0], preferred_element_type=F32) + bdn_ref[0]

    @pl.when(pl.program_id(0) >= nu_ref[0])
    def _():
        y_ref[...] = jnp.zeros_like(y_ref)


def _experts(xbuf, blk_e, n_used, wgu, bgu, wdn, bdn):
    nb = blk_e.shape[0]
    rb = xbuf.shape[0] // nb
    last = lambda i, nu: jnp.minimum(i, jnp.maximum(nu[0] - 1, 0))
    grid_spec = pltpu.PrefetchScalarGridSpec(
        num_scalar_prefetch=2,
        grid=(nb,),
        in_specs=[pl.BlockSpec((rb, D_MODEL), lambda i, be, nu: (last(i, nu), 0)),
                  pl.BlockSpec((1, D_MODEL, 2 * D_FF), lambda i, be, nu: (be[i], 0, 0)),
                  pl.BlockSpec((1, 1, 2 * D_FF), lambda i, be, nu: (be[i], 0, 0)),
                  pl.BlockSpec((1, D_FF, D_MODEL), lambda i, be, nu: (be[i], 0, 0)),
                  pl.BlockSpec((1, 1, D_MODEL), lambda i, be, nu: (be[i], 0, 0))],
        out_specs=pl.BlockSpec((rb, D_MODEL), lambda i, be, nu: (i, 0)),
    )
    return pl.pallas_call(
        _expert_body,
        grid_spec=grid_spec,
        out_shape=jax.ShapeDtypeStruct((nb * rb, D_MODEL), F32),
        compiler_params=_cparams(("arbitrary",)),
        name="experts",
    )(blk_e, n_used, xbuf, wgu, bgu.reshape(N_EXPERTS, 1, 2 * D_FF), wdn, bdn.reshape(N_EXPERTS, 1, D_MODEL))


def _moe(hf, x1, W):
    M = hf.shape[0]
    tm = min(MOE_TM, M)
    nt = M // tm
    assert M % tm == 0 and tm * TOP_K // SEG_ALIGN <= 256
    pos, gates, cnt = _router(hf, W['router_w'], W['router_b'], tm)

    rb = MOE_RB if M * TOP_K >= N_EXPERTS * MOE_RB else 128
    cpad = (cnt[:, 0, :N_EXPERTS] + (SEG_ALIGN - 1)) // SEG_ALIGN * SEG_ALIGN
    loff = jnp.cumsum(cpad, axis=1) - cpad
    region = (jnp.sum(cpad, axis=0) + rb - 1) // rb * rb
    pend = jnp.cumsum(region)
    gdst = (pend - region)[None, :] + jnp.cumsum(cpad, axis=0) - cpad
    max_rows = M * TOP_K + nt * N_EXPERTS * (SEG_ALIGN - 1)
    nb = (max_rows + rb - 1) // rb + N_EXPERTS
    blk_e = jnp.minimum(jnp.searchsorted(pend, jnp.arange(nb, dtype=I32) * rb, side='right'),
                        N_EXPERTS - 1).astype(I32)
    n_used = (pend[-1] // rb).astype(I32).reshape(1)
    plan = ((cpad // SEG_ALIGN).reshape(-1).astype(I32), loff.reshape(-1).astype(I32),
            gdst.reshape(-1).astype(I32))

    rows = (tm * TOP_K + N_EXPERTS * (SEG_ALIGN - 1) + 127) // 128 * 128
    nbits = (tm * TOP_K // SEG_ALIGN).bit_length()
    any_spec = pl.BlockSpec(memory_space=pl.ANY)
    row = lambda w: pl.BlockSpec((tm, w), lambda i, *_: (i, 0))
    xbuf = pl.pallas_call(
        functools.partial(_dispatch_body, tm=tm, rows=rows, nbits=nbits),
        grid_spec=pltpu.PrefetchScalarGridSpec(
            num_scalar_prefetch=3, grid=(nt,),
            in_specs=[row(128), row(D_MODEL), any_spec],
            out_specs=any_spec,
            scratch_shapes=[pltpu.VMEM((rows, D_MODEL), F32), pltpu.SemaphoreType.DMA(())]),
        out_shape=jax.ShapeDtypeStruct((nb * rb, D_MODEL), F32),
        input_output_aliases={5: 0},
        compiler_params=_cparams(("arbitrary",)),
        name="moe_dispatch",
    )(*plan, pos, hf, jnp.zeros((nb * rb, D_MODEL), F32))
    ybuf = _experts(xbuf, blk_e, n_used, W['exp_w_gu'], W['exp_b_gu'], W['exp_w_down'], W['exp_b_down'])
    return pl.pallas_call(
        functools.partial(_combine_body, tm=tm, rows=rows, nbits=nbits),
        grid_spec=pltpu.PrefetchScalarGridSpec(
            num_scalar_prefetch=3, grid=(nt,),
            in_specs=[row(128), row(128), row(D_MODEL), any_spec],
            out_specs=row(D_MODEL),
            scratch_shapes=[pltpu.VMEM((rows, D_MODEL), F32), pltpu.SemaphoreType.DMA(())]),
        out_shape=jax.ShapeDtypeStruct((M, D_MODEL), F32),
        compiler_params=_cparams(("arbitrary",)),
        name="moe_combine",
    )(*plan, pos, gates, x1, ybuf)


def _rmsnorm_ref(x, g):
    xf = x.astype(F32)
    return xf * lax.rsqrt(jnp.mean(xf * xf, axis=-1, keepdims=True) + EPS) * g


def _sparse_attention_xla(q, k_sel, v_sel, idx, q_pos, rel_bias):
    B, Q = q.shape[:2]
    group = N_HEADS // N_KV_HEADS
    qg = q.reshape(B, Q, N_KV_HEADS, group, HEAD_DIM)
    logits = jnp.einsum('bqcgd,bqjcd->bqcgj', qg, k_sel).astype(F32) * HEAD_DIM ** -0.5
    dist = q_pos[None, :, None] - idx
    bias = rel_bias[_t5_bucket(dist)].astype(F32)
    bias = bias.reshape(B, Q, -1, N_KV_HEADS, group).transpose(0, 1, 3, 4, 2)
    valid = (dist >= 0)[:, :, None, None, :]
    p = jax.nn.softmax(jnp.where(valid, logits + bias, -jnp.inf), axis=-1)
    out = jnp.einsum('bqcgj,bqjcd->bqcgd', p.astype(v_sel.dtype), v_sel)
    return out.reshape(B, Q, N_HEADS * HEAD_DIM)


def _dsa_sample_xla(q, k_new, v_new, qi, wi, ki_new, cache_k, cache_v, cache_kidx, page_table, rel_bias):
    Bd, S = q.shape[:2]
    past = page_table.shape[1] * PAGE_SIZE
    topk = min(TOPK, (past + S) // 4)
    ki_past = cache_kidx[page_table].reshape(Bd, past, IDX_DIM)
    ki_all = jnp.concatenate([ki_past, ki_new], axis=1)
    q_pos = past + jnp.arange(S, dtype=I32)
    s = jnp.einsum('bqhd,bsd->bqhs', qi, ki_all).astype(F32) * IDX_DIM ** -0.5
    scores = jnp.einsum('bqhs,bqh->bqs', jax.nn.relu(s), wi.astype(F32)) * N_IDX_HEADS ** -0.5
    s_pos = jnp.arange(scores.shape[-1], dtype=I32)
    masked = jnp.where(s_pos[None, None, :] <= q_pos[None, :, None], scores, -jnp.inf)
    _, idx = lax.top_k(masked, topk)
    lp = jnp.minimum(idx, past - 1)
    phys = jnp.take_along_axis(page_table, (lp // PAGE_SIZE).reshape(Bd, -1), axis=1).reshape(idx.shape)
    off = lp % PAGE_SIZE
    new_i = jnp.clip(idx - past, 0, S - 1)
    is_past = (idx < past)[..., None, None]
    gather_rows = jax.vmap(lambda rows, ii: rows[ii])
    k_sel = jnp.where(is_past, cache_k[phys, off], gather_rows(k_new, new_i))
    v_sel = jnp.where(is_past, cache_v[phys, off], gather_rows(v_new, new_i))
    return _sparse_attention_xla(q, k_sel, v_sel, idx, q_pos, rel_bias)


def _mem_attention_xla(xq, mk, mv):
    B, T = xq.shape[:2]
    logits = jnp.einsum('bthd,bmhd->bhtm', xq, mk).astype(F32) * X_HEAD_DIM ** -0.5
    p = jax.nn.softmax(logits, axis=-1)
    out = jnp.einsum('bhtm,bmhd->bthd', p, mv)
    return out.reshape(B, T, X_HEADS * X_HEAD_DIM)


def _pad_heads_cols(w):
    K = w.shape[0]
    w = w.reshape(K, X_HEADS, X_HEAD_DIM)
    return jnp.pad(w, ((0, 0), (0, 0), (0, X_HEAD_PAD - X_HEAD_DIM))).reshape(K, X_HEADS * X_HEAD_PAD)


def _pad_heads_vec(g):
    return jnp.pad(g, (0, X_HEAD_PAD - X_HEAD_DIM))


def _prep_weights(P):
    W = {}
    w_in = P['w_in'].astype(BF16)
    widths = (D_RNN, D_RNN, N_HEADS * HEAD_DIM, N_KV_HEADS * HEAD_DIM, N_KV_HEADS * HEAD_DIM,
              N_IDX_HEADS * IDX_DIM, IDX_DIM, N_IDX_HEADS, X_HEADS * X_HEAD_DIM, 3 * D_MODEL)
    offs = np.cumsum((0,) + widths)
    seg = lambda i: w_in[:, offs[i]:offs[i + 1]]
    W['w_xy'] = w_in[:, offs[0]:offs[2]]
    W['w_q'], W['w_k'], W['w_v'], W['w_qi'] = seg(2), seg(3), seg(4), seg(5)
    W['w_kiwi'] = jnp.pad(w_in[:, offs[6]:offs[8]], ((0, 0), (0, 128 - IDX_DIM - N_IDX_HEADS)))
    W['w_xq'] = _pad_heads_cols(seg(8))
    W['w_g'] = seg(9)
    W['q_gain'] = jnp.tile(P['q_norm'], N_HEADS)
    W['q_gain_log2'] = W['q_gain'] * (HEAD_DIM ** -0.5 * LOG2E)
    W['k_gain'] = jnp.tile(P['k_norm'], N_KV_HEADS)
    W['xq_gain'] = jnp.tile(_pad_heads_vec(P['xq_norm']), X_HEADS)
    W['xk_gain'] = jnp.tile(_pad_heads_vec(P['xk_norm']), X_HEADS)
    wmk, wmv = jnp.split(P['w_mem_kv'].astype(BF16), 2, axis=-1)
    W['w_mem_k'] = _pad_heads_cols(wmk)
    W['w_mem_v'] = _pad_heads_cols(wmv)
    W['wg_lru'] = _block_diag_gates(P['lru_wa'], P['lru_wx'])
    W['w_br_rnn'] = P['w_br_rnn'].astype(BF16)
    W['w_br_attn'] = P['w_br_attn'].astype(BF16)
    W['w_br_mem'] = _pad_heads_cols(P['w_br_mem'].astype(BF16).T).T
    W['w_out'] = P['w_out'].astype(BF16)
    W['router_w'] = jnp.pad(P['router_w'].astype(BF16), ((0, 0), (0, 128 - N_EXPERTS)))
    W['router_b'] = jnp.pad(P['router_b'].astype(F32), (0, 128 - N_EXPERTS), constant_values=-1e30).reshape(1, 128)
    W['exp_w_gu'] = P['exp_w_gu'].astype(BF16)
    W['exp_w_down'] = P['exp_w_down'].astype(BF16)
    for name in ('norm_mix', 'conv_w', 'conv_b', 'lru_ba', 'lru_bx', 'lru_lambda', 'rel_bias', 'mem_norm',
                 'norm_ffn', 'exp_b_gu', 'exp_b_down'):
        W[name] = P[name]
    return W


def _unpad_heads(a):
    return a.reshape(a.shape[:-1] + (X_HEADS, X_HEAD_PAD))[..., :X_HEAD_DIM]


def _in_proj(x2, W, q_gain):
    g = W['norm_mix']
    o = {}
    (o['xy'],) = _norm_proj(x2, g, W['w_xy'], [F32], tn=1024)
    (o['q'],) = _norm_proj(x2, g, W['w_q'], [BF16], tn=512, mode="headnorm", head_gain=q_gain)
    o['k'], o['k16'] = _norm_proj(x2, g, W['w_k'], [F32, BF16], tn=512, mode="headnorm", head_gain=W['k_gain'])
    o['v'], o['v16'] = _norm_proj(x2, g, W['w_v'], [F32, BF16], tn=512)
    (o['qi'],) = _norm_proj(x2, g, W['w_qi'], [BF16], tn=512)
    (o['kiwi'],) = _norm_proj(x2, g, W['w_kiwi'], [F32], tn=128)
    (o['xq'],) = _norm_proj(x2, g, W['w_xq'], [BF16], tn=512, mode="headnorm", head_gain=W['xq_gain'],
                            hd_pad=X_HEAD_PAD, hd_true=X_HEAD_DIM)
    (o['g'],) = _norm_proj(x2, g, W['w_g'], [F32], tn=1024)
    return o


def kernel(x_prompt, x_sample, cache_k, cache_v, cache_kidx, cache_mem_k, cache_mem_v, state_conv, state_rglru, page_table, mem_prompt, norm_mix, w_in, conv_w, conv_b, lru_wa, lru_ba, lru_wx, lru_bx, lru_lambda, q_norm, k_norm, rel_bias, mem_norm, w_mem_kv, xq_norm, xk_norm, w_br_rnn, w_br_attn, w_br_mem, w_out, norm_ffn, router_w, router_b, exp_w_gu, exp_b_gu, exp_w_down, exp_b_down):
    W = _prep_weights(dict(norm_mix=norm_mix, w_in=w_in, conv_w=conv_w, conv_b=conv_b, lru_wa=lru_wa,
                           lru_ba=lru_ba, lru_wx=lru_wx, lru_bx=lru_bx, lru_lambda=lru_lambda, q_norm=q_norm,
                           k_norm=k_norm, rel_bias=rel_bias, mem_norm=mem_norm, w_mem_kv=w_mem_kv,
                           xq_norm=xq_norm, xk_norm=xk_norm, w_br_rnn=w_br_rnn, w_br_attn=w_br_attn,
                           w_br_mem=w_br_mem, w_out=w_out, norm_ffn=norm_ffn, router_w=router_w,
                           router_b=router_b, exp_w_gu=exp_w_gu, exp_b_gu=exp_b_gu, exp_w_down=exp_w_down,
                           exp_b_down=exp_b_down))
    lru = (W['conv_w'], W['conv_b'], W['wg_lru'], W['lru_ba'], W['lru_bx'], W['lru_lambda'])

    B, T, D = x_prompt.shape
    xp2 = x_prompt.reshape(B * T, D)
    memp2 = mem_prompt.reshape(B * N_MEM, D)
    mk_pad, mk16 = _norm_proj(memp2, W['mem_norm'], W['w_mem_k'], [F32, BF16], tn=512, mode="headnorm",
                              head_gain=W['xk_gain'], hd_pad=X_HEAD_PAD, hd_true=X_HEAD_DIM)
    mv_pad, mv16 = _norm_proj(memp2, W['mem_norm'], W['w_mem_v'], [F32, BF16], tn=512)
    mk_p = _unpad_heads(mk_pad).reshape(B, N_MEM, X_HEADS, X_HEAD_DIM)
    mv_p = _unpad_heads(mv_pad).reshape(B, N_MEM, X_HEADS, X_HEAD_DIM)

    o = _in_proj(xp2, W, W['q_gain_log2'])
    xy = o['xy'].reshape(B, T, 2 * D_RNN)
    rnn_out, rg_p = _rglru_prompt(xy, *lru)
    conv_p = xy[:, T - (CONV_W - 1):, :D_RNN]
    ki_p = o['kiwi'][:, :IDX_DIM].reshape(B, T, IDX_DIM)
    wi_p = o['kiwi'][:, IDX_DIM:IDX_DIM + N_IDX_HEADS].reshape(B, T, N_IDX_HEADS)
    qi_h = o['qi'].reshape(B, T, N_IDX_HEADS, IDX_DIM).transpose(0, 2, 1, 3)
    kit = ki_p.astype(BF16).transpose(0, 2, 1)
    attn_out = _dsa_prompt(o['q'].reshape(B, T, -1), o['k16'].reshape(B, T, -1), o['v16'].reshape(B, T, -1),
                           qi_h, wi_p, kit, W['rel_bias'])
    mem_out = _mem_attn(o['xq'].reshape(B, T, -1), mk16.reshape(B, N_MEM, -1), mv16.reshape(B, N_MEM, -1))
    x1, hf = _merge(xp2, rnn_out.reshape(B * T, -1), attn_out.reshape(B * T, -1), mem_out.reshape(B * T, -1),
                    o['g'], W['w_br_rnn'], W['w_br_attn'], W['w_br_mem'], W['w_out'], W['norm_ffn'])
    y_prompt = _moe(hf, x1, W).reshape(B, T, D)
    k_p = o['k'].reshape(B, T, N_KV_HEADS, HEAD_DIM)
    v_p = o['v'].reshape(B, T, N_KV_HEADS, HEAD_DIM)

    Bd, S, _ = x_sample.shape
    xs2 = x_sample.reshape(Bd * S, D)
    os_ = _in_proj(xs2, W, W['q_gain'])
    rnn_s, rg_s = _rglru_step(os_['xy'], state_conv.transpose(1, 0, 2), state_rglru, *lru)
    conv_s = jnp.concatenate([state_conv[:, 1:], os_['xy'][:, None, :D_RNN]], axis=1)
    k_s = os_['k'].reshape(Bd, S, N_KV_HEADS, HEAD_DIM)
    v_s = os_['v'].reshape(Bd, S, N_KV_HEADS, HEAD_DIM)
    ki_s = os_['kiwi'][:, :IDX_DIM].reshape(Bd, S, IDX_DIM)
    wi_s = os_['kiwi'][:, IDX_DIM:IDX_DIM + N_IDX_HEADS].reshape(Bd, S, N_IDX_HEADS)
    attn_s = _dsa_sample_xla(os_['q'].astype(F32).reshape(Bd, S, N_HEADS, HEAD_DIM), k_s, v_s,
                             os_['qi'].astype(F32).reshape(Bd, S, N_IDX_HEADS, IDX_DIM), wi_s, ki_s,
                             cache_k, cache_v, cache_kidx, page_table, W['rel_bias'])
    xq_s = _unpad_heads(os_['xq'].astype(F32)).reshape(Bd, S, X_HEADS, X_HEAD_DIM)
    mem_s = _mem_attention_xla(xq_s, cache_mem_k, cache_mem_v)
    mem_s = jnp.pad(mem_s.reshape(Bd * S, X_HEADS, X_HEAD_DIM),
                    ((0, 0), (0, 0), (0, X_HEAD_PAD - X_HEAD_DIM))).reshape(Bd * S, -1)
    x1s, hfs = _merge(xs2, rnn_s, attn_s.reshape(Bd * S, -1).astype(BF16), mem_s.astype(BF16), os_['g'],
                      W['w_br_rnn'], W['w_br_attn'], W['w_br_mem'], W['w_out'], W['norm_ffn'])
    y_sample = _moe(hfs, x1s, W).reshape(Bd, S, D)

    return (y_prompt, y_sample, k_p, v_p, ki_p, mk_p, mv_p, conv_p, rg_p.reshape(B, D_RNN),
            k_s, v_s, ki_s, conv_s, rg_s)
```

```python
import functools
import math

import numpy as np
import jax
import jax.numpy as jnp
from jax import lax
from jax.experimental import pallas as pl
from jax.experimental.pallas import tpu as pltpu

F32 = jnp.float32
BF16 = jnp.bfloat16
I32 = jnp.int32

D_MODEL = 1024
D_RNN = 1024
LRU_BLOCKS = 16
LRU_BW = 64
CONV_W = 4
LRU_C = 8.0
N_HEADS = 8
N_KV_HEADS = 4
HEAD_DIM = 128
N_IDX_HEADS = 8
IDX_DIM = 64
TOPK = 256
PAGE_SIZE = 128
N_MEM = 256
X_HEADS = 4
X_HEAD_DIM = 192
X_HEAD_PAD = 256
N_BUCKETS = 32
MAX_DISTANCE = 128
N_EXPERTS = 32
TOP_K = 4
D_FF = 1024
SWIGLU_LIMIT = 7.0
SWIGLU_ALPHA = 1.702
EPS = 1e-6

INT_MIN = -2 ** 31
NEG = -1e30
LOG2E = 1.4426950408889634
VMEM_LIMIT = 56 * 1024 * 1024

DSA_TQ = 256
DSA_KB = 1024
DEC_G = 16
MOE_RB = 256
MOE_TM = 256
SEG_ALIGN = 8


def _cparams(sem):
    return pltpu.CompilerParams(dimension_semantics=sem, vmem_limit_bytes=VMEM_LIMIT)


def _sigmoid(x):
    return 1.0 / (1.0 + jnp.exp(-x))


def _gelu_tanh(x):
    return 0.5 * x * (1.0 + jnp.tanh(0.7978845608028654 * (x + 0.044715 * x * x * x)))


def _head_norm(y, gain, hd_pad, hd_true):
    parts = []
    for s in range(0, y.shape[1], hd_pad):
        ys = y[:, s:s + hd_pad]
        ms = jnp.sum(ys * ys, axis=-1, keepdims=True) * (1.0 / hd_true)
        parts.append(ys * lax.rsqrt(ms + EPS) * gain[:, s:s + hd_pad])
    return parts[0] if len(parts) == 1 else jnp.concatenate(parts, axis=-1)


def _proj_body(x_ref, g_ref, w_ref, *refs, mode, hd_pad, hd_true, n_out):
    if mode == "headnorm":
        gain_ref, refs = refs[0], refs[1:]
    outs, h_ref = refs[:n_out], refs[n_out]

    @pl.when(pl.program_id(1) == 0)
    def _():
        x = x_ref[...]
        ms = jnp.mean(x * x, axis=-1, keepdims=True)
        h_ref[...] = (x * lax.rsqrt(ms + EPS) * g_ref[...]).astype(BF16)

    y = jnp.dot(h_ref[...], w_ref[...], preferred_element_type=F32)
    if mode == "headnorm":
        y = _head_norm(y, gain_ref[...], hd_pad, hd_true)
    for o in outs:
        o[...] = y.astype(o.dtype)


def _norm_proj(x, gain, w, out_dtypes, *, tn, mode="plain", head_gain=None, hd_pad=128, hd_true=128):
    M, K = x.shape
    N = w.shape[1]
    tm = min(512, M)
    assert M % tm == 0 and N % tn == 0
    in_specs = [pl.BlockSpec((tm, K), lambda i, j: (i, 0)),
                pl.BlockSpec((1, K), lambda i, j: (0, 0)),
                pl.BlockSpec((K, tn), lambda i, j: (0, j))]
    args = [x, gain.reshape(1, K), w]
    if mode == "headnorm":
        in_specs.append(pl.BlockSpec((1, tn), lambda i, j: (0, j)))
        args.append(head_gain.reshape(1, N))
    body = functools.partial(_proj_body, mode=mode, hd_pad=hd_pad, hd_true=hd_true, n_out=len(out_dtypes))
    return pl.pallas_call(
        body,
        grid=(M // tm, N // tn),
        in_specs=in_specs,
        out_specs=[pl.BlockSpec((tm, tn), lambda i, j: (i, j)) for _ in out_dtypes],
        out_shape=[jax.ShapeDtypeStruct((M, N), dt) for dt in out_dtypes],
        scratch_shapes=[pltpu.VMEM((tm, K), BF16)],
        compiler_params=_cparams(("parallel", "arbitrary")),
        name="norm_proj_" + mode,
    )(*args)


def _softplus(z):
    return jnp.maximum(z, 0.0) + jnp.log1p(jnp.exp(-jnp.abs(z)))


def _lru_terms(xc, wg_ref, ba, bx, lam, a_out, b_out):
    sp = _softplus(-lam)
    for gi in range(4):
        sl = slice(gi * 256, (gi + 1) * 256)
        xg = xc[:, sl]
        z = jnp.dot(xg.astype(BF16), wg_ref[gi], preferred_element_type=F32)
        r = _sigmoid(z[:, :256] + ba[:, sl])
        ig = _sigmoid(z[:, 256:] + bx[:, sl])
        log_a = -LRU_C * r * sp[:, sl]
        a_out[:, sl] = jnp.exp(log_a)
        b_out[:, sl] = jnp.sqrt(1.0 - jnp.exp(2.0 * log_a)) * ig * xg


def _rglru_prompt_body(xr_ref, yr_ref, cw_ref, cb_ref, wg_ref, ba_ref, bx_ref, lam_ref,
                       out_ref, hl_ref, xp_ref, a_ref, b_ref, h_ref, hc_ref, *, tc):
    t = pl.program_id(1)

    @pl.when(t == 0)
    def _():
        xp_ref[0:8, :] = jnp.zeros((8, D_RNN), F32)
        hc_ref[...] = jnp.zeros_like(hc_ref)

    @pl.when(t > 0)
    def _():
        xp_ref[0:8, :] = xp_ref[tc:tc + 8, :]

    xp_ref[8:8 + tc, :] = xr_ref[0]
    cw = cw_ref[...]
    xc = cb_ref[...] + cw[0:1] * xp_ref[5:5 + tc, :]
    xc = xc + cw[1:2] * xp_ref[6:6 + tc, :]
    xc = xc + cw[2:3] * xp_ref[7:7 + tc, :]
    xc = xc + cw[3:4] * xp_ref[8:8 + tc, :]
    _lru_terms(xc, wg_ref, ba_ref[...], bx_ref[...], lam_ref[...], a_ref, b_ref)

    def step(i, h):
        h = a_ref[pl.ds(i, 1), :] * h + b_ref[pl.ds(i, 1), :]
        h_ref[pl.ds(i, 1), :] = h
        return h

    h = lax.fori_loop(0, tc, step, hc_ref[0:1, :], unroll=8)
    hc_ref[0:1, :] = h
    out_ref[0] = (h_ref[...] * _gelu_tanh(yr_ref[0])).astype(out_ref.dtype)

    @pl.when(t == pl.num_programs(1) - 1)
    def _():
        hl_ref[0] = h


def _rglru_prompt(xy, conv_w, conv_b, wg, ba, bx, lam):
    B, T, _ = xy.shape
    tc = min(256, T)
    vec = lambda a: a.reshape(1, D_RNN)
    full = lambda shape: pl.BlockSpec(shape, lambda b, t: (0,) * len(shape))
    return pl.pallas_call(
        functools.partial(_rglru_prompt_body, tc=tc),
        grid=(B, T // tc),
        in_specs=[pl.BlockSpec((1, tc, D_RNN), lambda b, t: (b, t, 0)),
                  pl.BlockSpec((1, tc, D_RNN), lambda b, t: (b, t, 1)),
                  full((CONV_W, D_RNN)), full((1, D_RNN)), full((4, 256, 512)),
                  full((1, D_RNN)), full((1, D_RNN)), full((1, D_RNN))],
        out_specs=[pl.BlockSpec((1, tc, D_RNN), lambda b, t: (b, t, 0)),
                   pl.BlockSpec((1, 1, D_RNN), lambda b, t: (b, 0, 0))],
        out_shape=[jax.ShapeDtypeStruct((B, T, D_RNN), BF16),
                   jax.ShapeDtypeStruct((B, 1, D_RNN), F32)],
        scratch_shapes=[pltpu.VMEM((tc + 8, D_RNN), F32), pltpu.VMEM((tc, D_RNN), F32),
                        pltpu.VMEM((tc, D_RNN), F32), pltpu.VMEM((tc, D_RNN), F32),
                        pltpu.VMEM((8, D_RNN), F32)],
        compiler_params=_cparams(("parallel", "arbitrary")),
        name="rglru_prompt",
    )(xy, xy, conv_w, vec(conv_b), wg, vec(ba), vec(bx), vec(lam))


def _rglru_step_body(xr_ref, yr_ref, prev_ref, h0_ref, cw_ref, cb_ref, wg_ref, ba_ref, bx_ref, lam_ref,
                     out_ref, hn_ref, a_ref, b_ref):
    cw = cw_ref[...]
    xc = cb_ref[...] + cw[0:1] * prev_ref[0] + cw[1:2] * prev_ref[1] + cw[2:3] * prev_ref[2] + cw[3:4] * xr_ref[...]
    _lru_terms(xc, wg_ref, ba_ref[...], bx_ref[...], lam_ref[...], a_ref, b_ref)
    h = a_ref[...] * h0_ref[...] + b_ref[...]
    hn_ref[...] = h
    out_ref[...] = (h * _gelu_tanh(yr_ref[...])).astype(out_ref.dtype)


def _rglru_step(xy, prev, h0, conv_w, conv_b, wg, ba, bx, lam):
    R = xy.shape[0]
    vec = lambda a: a.reshape(1, D_RNN)
    full = lambda shape: pl.BlockSpec(shape, lambda i: (0,) * len(shape))
    return pl.pallas_call(
        _rglru_step_body,
        grid=(1,),
        in_specs=[pl.BlockSpec((R, D_RNN), lambda i: (0, 0)), pl.BlockSpec((R, D_RNN), lambda i: (0, 1)),
                  full((CONV_W - 1, R, D_RNN)), full((R, D_RNN)),
                  full((CONV_W, D_RNN)), full((1, D_RNN)), full((4, 256, 512)),
                  full((1, D_RNN)), full((1, D_RNN)), full((1, D_RNN))],
        out_specs=[full((R, D_RNN)), full((R, D_RNN))],
        out_shape=[jax.ShapeDtypeStruct((R, D_RNN), BF16), jax.ShapeDtypeStruct((R, D_RNN), F32)],
        scratch_shapes=[pltpu.VMEM((R, D_RNN), F32), pltpu.VMEM((R, D_RNN), F32)],
        compiler_params=_cparams(("arbitrary",)),
        name="rglru_step",
    )(xy, xy, prev, h0, conv_w, vec(conv_b), wg, vec(ba), vec(bx), vec(lam))


def _block_diag_gates(wa, wx):
    def bd(w):
        w4 = w.reshape(4, 4, LRU_BW, LRU_BW)
        return jnp.einsum('gaij,ab->gaibj', w4, jnp.eye(4, dtype=w.dtype)).reshape(4, 256, 256)
    return jnp.concatenate([bd(wa), bd(wx)], axis=-1).astype(BF16)


def _sortable_key(s):
    bits = pltpu.bitcast(s + 0.0, I32)
    return jnp.where(bits >= 0, bits, bits ^ jnp.int32(0x7FFFFFFF))


def _topk_threshold(count_ge, rows):
    cnt0 = count_ge(jnp.zeros((rows, 1), I32))
    pos = cnt0 >= TOPK
    base = jnp.where(pos, jnp.int32(0), jnp.int32(INT_MIN))
    cnt = jnp.where(pos, cnt0, jnp.int32(TOPK))

    def bit_body(i, carry):
        base, cnt = carry
        cand = base | jnp.left_shift(jnp.int32(1), jnp.int32(30) - i)
        c = count_ge(cand)
        ok = c >= TOPK
        return jnp.where(ok, cand, base), jnp.where(ok, c, cnt)

    return lax.fori_loop(0, 31, bit_body, (base, cnt))


def _attend_block(q_ref, k, v, mask, bias_of_head, m_ref, l_ref, acc_ref):
    for h in range(N_HEADS):
        c = h // (N_HEADS // N_KV_HEADS)
        hs = slice(h * HEAD_DIM, (h + 1) * HEAD_DIM)
        cs = slice(c * HEAD_DIM, (c + 1) * HEAD_DIM)
        s = lax.dot_general(q_ref[0, :, hs], k[:, cs], (((1,), (1,)), ((), ())),
                            preferred_element_type=F32)
        bias = bias_of_head(h)
        s = jnp.where(mask, s if bias is None else s + bias, NEG)
        m_old = m_ref[h]
        m_new = jnp.maximum(m_old, jnp.max(s, axis=-1, keepdims=True))
        alpha = jnp.exp2(m_old - m_new)
        p = jnp.exp2(s - m_new)
        l_ref[h] = alpha * l_ref[h] + jnp.sum(p, axis=-1, keepdims=True)
        acc_ref[:, hs] = alpha * acc_ref[:, hs] + jnp.dot(p.astype(BF16), v[:, cs],
                                                          preferred_element_type=F32)
        m_ref[h] = m_new


def _dsa_prompt_body(qi_ref, wi_ref, kit_ref, q_ref, kf_ref, vf_ref, kp_ref, vp_ref,
                     kd_ref, vd_ref, tbd_ref, tbp_ref, out_ref,
                     s_ref, thr_ref, m_ref, l_ref, acc_ref, *, tq, kb):
    qb = pl.program_id(1)
    st = pl.program_id(2)
    t0 = qb * tq
    far_len = jnp.maximum(qb - 1, 0) * tq
    n_far = (far_len + kb - 1) // kb

    @pl.when(st == 0)
    def _():
        m_ref[...] = jnp.full(m_ref.shape, NEG, F32)
        l_ref[...] = jnp.zeros_like(l_ref)
        acc_ref[...] = jnp.zeros_like(acc_ref)

        nkc = (t0 + tq + kb - 1) // kb
        wi = wi_ref[0]
        row = t0 + lax.broadcasted_iota(I32, (tq, kb), 0)
        lane = lax.broadcasted_iota(I32, (tq, kb), 1)

        def score_chunk(c, carry):
            off = pl.multiple_of(c * kb, kb)
            kic = kit_ref[0, :, pl.ds(off, kb)]
            acc = jnp.zeros((tq, kb), F32)
            for h in range(N_IDX_HEADS):
                s = jnp.dot(qi_ref[0, h], kic, preferred_element_type=F32)
                acc = acc + jnp.maximum(s, 0.0) * wi[:, h:h + 1]
            key = jnp.where(off + lane <= row, _sortable_key(acc), jnp.int32(INT_MIN))
            s_ref[:, pl.ds(off, kb)] = key
            return carry

        lax.fori_loop(0, nkc, score_chunk, 0)

        def count_ge(cand):
            def body(c, part):
                off = pl.multiple_of(c * kb, kb)
                ge = (s_ref[:, pl.ds(off, kb)] >= cand).astype(I32)
                for j in range(kb // 128):
                    part = part + ge[:, j * 128:(j + 1) * 128]
                return part
            part = lax.fori_loop(0, nkc, body, jnp.zeros((tq, 128), I32))
            return jnp.sum(part, axis=-1, keepdims=True)

        base, cnt_base = _topk_threshold(count_ge, tq)
        thr = jnp.maximum(base, jnp.int32(INT_MIN + 1))
        thr_ref[...] = thr

        tie = jnp.logical_and(base > INT_MIN, cnt_base > TOPK)

        @pl.when(jnp.max(tie.astype(I32)) > 0)
        def _():
            n_gt = count_ge(base + 1)
            need = jnp.where(tie, TOPK - n_gt, jnp.int32(2 ** 30)).astype(F32)
            sub = 256
            incl = (lax.broadcasted_iota(I32, (sub, sub), 0) <= lax.broadcasted_iota(I32, (sub, sub), 1)).astype(BF16)

            def tie_chunk(c, seen):
                for j in range(kb // sub):
                    off = pl.multiple_of(c * kb + j * sub, sub)
                    blk = s_ref[:, pl.ds(off, sub)]
                    eq = (blk == base).astype(F32)
                    rank = jnp.dot(eq.astype(BF16), incl, preferred_element_type=F32) + seen
                    demote = eq * (rank - need) > 0.0
                    s_ref[:, pl.ds(off, sub)] = jnp.where(demote, base - 1, blk)
                    seen = seen + jnp.sum(eq, axis=-1, keepdims=True)
                return seen

            lax.fori_loop(0, nkc, tie_chunk, jnp.zeros((tq, 1), F32))

        @pl.when(qb > 0)
        def _():
            off = pl.multiple_of(t0 - tq, tq)
            mask = s_ref[:, pl.ds(off, tq)] >= thr
            _attend_block(q_ref, kp_ref[0], vp_ref[0], mask, lambda h: tbp_ref[h], m_ref, l_ref, acc_ref)

        off = pl.multiple_of(t0, tq)
        mask = s_ref[:, pl.ds(off, tq)] >= thr
        _attend_block(q_ref, kd_ref[0], vd_ref[0], mask, lambda h: tbd_ref[h], m_ref, l_ref, acc_ref)

    @pl.when(jnp.logical_and(st > 0, st - 1 < n_far))
    def _():
        off = pl.multiple_of((st - 1) * kb, kb)
        col = off + lax.broadcasted_iota(I32, (tq, kb), 1)
        keys = jnp.where(col < far_len, s_ref[:, pl.ds(off, kb)], jnp.int32(INT_MIN))
        _attend_block(q_ref, kf_ref[0], vf_ref[0], keys >= thr_ref[...], lambda h: None, m_ref, l_ref, acc_ref)

    @pl.when(st == pl.num_programs(2) - 1)
    def _():
        for h in range(N_HEADS):
            hs = slice(h * HEAD_DIM, (h + 1) * HEAD_DIM)
            out_ref[0, :, hs] = (acc_ref[:, hs] / l_ref[h]).astype(out_ref.dtype)


def _t5_bucket(dist):
    n = jnp.maximum(dist, 0)
    max_exact = N_BUCKETS // 2
    nf = jnp.maximum(n, 1).astype(F32)
    large = max_exact + (jnp.log(nf / max_exact) / math.log(MAX_DISTANCE / max_exact)
                         * (N_BUCKETS - max_exact)).astype(I32)
    large = jnp.minimum(large, N_BUCKETS - 1)
    return jnp.where(n < max_exact, n, large)


_FAR_DIST = int(math.ceil((N_BUCKETS // 2) * (MAX_DISTANCE / (N_BUCKETS // 2)) **
                          ((N_BUCKETS - 1 - N_BUCKETS // 2) / (N_BUCKETS - N_BUCKETS // 2)))) + 1


def _dsa_prompt(q, k, v, qi_h, wi, kit, rel_bias):
    B, T, _ = q.shape
    tq = min(DSA_TQ, T)
    kb = min(DSA_KB, T)
    assert T % tq == 0 and T % kb == 0 and kb % tq == 0 and tq >= _FAR_DIST and T >= 4 * TOPK
    nq = T // tq
    n_far_max = max(1, ((nq - 2) * tq + kb - 1) // kb) if nq > 1 else 1
    d = jnp.arange(tq, dtype=I32)[:, None] - jnp.arange(tq, dtype=I32)[None, :]
    rel = (rel_bias - rel_bias[N_BUCKETS - 1]).astype(F32) * LOG2E
    tb_diag = rel[_t5_bucket(d)].transpose(2, 0, 1)
    tb_prev = rel[_t5_bucket(d + tq)].transpose(2, 0, 1)

    def far_idx(b, qb, st):
        far_len = jnp.maximum(qb - 1, 0) * tq
        n_far = (far_len + kb - 1) // kb
        return (b, jnp.clip(st - 1, 0, jnp.maximum(n_far - 1, 0)), 0)

    kv_far = pl.BlockSpec((1, kb, N_KV_HEADS * HEAD_DIM), far_idx)
    kv_prev = pl.BlockSpec((1, tq, N_KV_HEADS * HEAD_DIM), lambda b, qb, st: (b, jnp.maximum(qb - 1, 0), 0))
    kv_diag = pl.BlockSpec((1, tq, N_KV_HEADS * HEAD_DIM), lambda b, qb, st: (b, qb, 0))
    tile = pl.BlockSpec((N_HEADS, tq, tq), lambda b, qb, st: (0, 0, 0))
    return pl.pallas_call(
        functools.partial(_dsa_prompt_body, tq=tq, kb=kb),
        grid=(B, nq, n_far_max + 1),
        in_specs=[pl.BlockSpec((1, N_IDX_HEADS, tq, IDX_DIM), lambda b, qb, st: (b, 0, qb, 0)),
                  pl.BlockSpec((1, tq, N_IDX_HEADS), lambda b, qb, st: (b, qb, 0)),
                  pl.BlockSpec((1, IDX_DIM, T), lambda b, qb, st: (b, 0, 0)),
                  pl.BlockSpec((1, tq, N_HEADS * HEAD_DIM), lambda b, qb, st: (b, qb, 0)),
                  kv_far, kv_far, kv_prev, kv_prev, kv_diag, kv_diag, tile, tile],
        out_specs=pl.BlockSpec((1, tq, N_HEADS * HEAD_DIM), lambda b, qb, st: (b, qb, 0)),
        out_shape=jax.ShapeDtypeStruct((B, T, N_HEADS * HEAD_DIM), BF16),
        scratch_shapes=[pltpu.VMEM((tq, T), I32), pltpu.VMEM((tq, 1), I32),
                        pltpu.VMEM((N_HEADS, tq, 1), F32), pltpu.VMEM((N_HEADS, tq, 1), F32),
                        pltpu.VMEM((tq, N_HEADS * HEAD_DIM), F32)],
        compiler_params=_cparams(("parallel", "arbitrary", "arbitrary")),
        name="dsa_prompt",
    )(qi_h, wi, kit, q, k, v, k, v, k, v, tb_diag, tb_prev)


def _mem_attn_body(xq_ref, mk_ref, mv_ref, out_ref):
    scale = X_HEAD_DIM ** -0.5
    for h in range(X_HEADS):
        hs = slice(h * X_HEAD_PAD, (h + 1) * X_HEAD_PAD)
        s = lax.dot_general(xq_ref[0, :, hs], mk_ref[0, :, hs], (((1,), (1,)), ((), ())),
                            preferred_element_type=F32) * scale
        m = jnp.max(s, axis=-1, keepdims=True)
        p = jnp.exp(s - m)
        p = p / jnp.sum(p, axis=-1, keepdims=True)
        out_ref[0, :, hs] = jnp.dot(p.astype(BF16), mv_ref[0, :, hs],
                                    preferred_element_type=F32).astype(out_ref.dtype)


def _mem_attn(xq, mk, mv):
    B, T, W = xq.shape
    M = mk.shape[1]
    tm = min(512, T)
    return pl.pallas_call(
        _mem_attn_body,
        grid=(B, T // tm),
        in_specs=[pl.BlockSpec((1, tm, W), lambda b, i: (b, i, 0)),
                  pl.BlockSpec((1, M, W), lambda b, i: (b, 0, 0)),
                  pl.BlockSpec((1, M, W), lambda b, i: (b, 0, 0))],
        out_specs=pl.BlockSpec((1, tm, W), lambda b, i: (b, i, 0)),
        out_shape=jax.ShapeDtypeStruct((B, T, W), BF16),
        compiler_params=_cparams(("parallel", "parallel")),
        name="mem_attn",
    )(xq, mk, mv)


def _merge_body(x_ref, rnn_ref, att_ref, mem_ref, g_ref, wr_ref, wa_ref, wm_ref, wo_ref, gn_ref,
                x1_ref, hf_ref):
    mixed = _sigmoid(g_ref[:, 0:D_MODEL]) * jnp.dot(rnn_ref[...], wr_ref[...], preferred_element_type=F32)
    mixed = mixed + _sigmoid(g_ref[:, D_MODEL:2 * D_MODEL]) * jnp.dot(att_ref[...], wa_ref[...],
                                                                     preferred_element_type=F32)
    mixed = mixed + _sigmoid(g_ref[:, 2 * D_MODEL:3 * D_MODEL]) * jnp.dot(mem_ref[...], wm_ref[...],
                                                                         preferred_element_type=F32)
    x1 = x_ref[...] + jnp.dot(mixed.astype(BF16), wo_ref[...], preferred_element_type=F32)
    x1_ref[...] = x1
    ms = jnp.mean(x1 * x1, axis=-1, keepdims=True)
    hf_ref[...] = (x1 * lax.rsqrt(ms + EPS) * gn_ref[...]).astype(BF16)


def _merge(x, rnn, att, mem, g, wr, wa, wm, wo, gain):
    M = x.shape[0]
    tm = min(256, M)
    row = lambda w: pl.BlockSpec((tm, w), lambda i: (i, 0))
    full = lambda a: pl.BlockSpec(a.shape, lambda i: (0,) * a.ndim)
    gain = gain.reshape(1, D_MODEL)
    return pl.pallas_call(
        _merge_body,
        grid=(M // tm,),
        in_specs=[row(D_MODEL), row(rnn.shape[1]), row(att.shape[1]), row(mem.shape[1]), row(3 * D_MODEL),
                  full(wr), full(wa), full(wm), full(wo), full(gain)],
        out_specs=[row(D_MODEL), row(D_MODEL)],
        out_shape=[jax.ShapeDtypeStruct((M, D_MODEL), F32), jax.ShapeDtypeStruct((M, D_MODEL), BF16)],
        compiler_params=_cparams(("parallel",)),
        name="merge",
    )(x, rnn, att, mem, g, wr, wa, wm, wo, gain)


def _router_body(h_ref, w_ref, b_ref, pos_ref, gate_ref, cnt_ref, *, tm):
    logits = jnp.dot(h_ref[...], w_ref[...], preferred_element_type=F32) + b_ref[...]
    lane = lax.broadcasted_iota(I32, logits.shape, 1)
    vals, idxs = [], []
    for _ in range(TOP_K):
        m = jnp.max(logits, axis=-1, keepdims=True)
        i = jnp.min(jnp.where(logits == m, lane, jnp.int32(128)), axis=-1, keepdims=True)
        vals.append(m)
        idxs.append(i)
        logits = jnp.where(lane == i, -jnp.inf, logits)
    es = [jnp.exp(v - vals[0]) for v in vals]
    tot = es[0] + es[1] + es[2] + es[3]

    hots = [lane == i for i in idxs]
    hotf = [h.astype(F32) for h in hots]
    cnts = [jnp.sum(h, axis=0, keepdims=True) for h in hotf]
    c_i = (cnts[0] + cnts[1] + cnts[2] + cnts[3]).astype(I32)
    cpad = (((c_i + (SEG_ALIGN - 1)) // SEG_ALIGN) * SEG_ALIGN).astype(F32)
    r128 = lax.broadcasted_iota(I32, (128, 128), 0)
    c128 = lax.broadcasted_iota(I32, (128, 128), 1)
    loff = jnp.dot(jnp.broadcast_to(cpad, (8, 128)).astype(BF16), (r128 < c128).astype(BF16),
                   preferred_element_type=F32)[0:1]
    low = (lax.broadcasted_iota(I32, (tm, tm), 0) > lax.broadcasted_iota(I32, (tm, tm), 1)).astype(BF16)
    before = loff
    po = jnp.zeros(lane.shape, I32)
    go = jnp.zeros(lane.shape, F32)
    for k in range(TOP_K):
        pref = jnp.dot(low, hotf[k].astype(BF16), preferred_element_type=F32)
        pos = jnp.sum(jnp.where(hots[k], before + pref, 0.0), axis=-1, keepdims=True)
        before = before + cnts[k]
        po = jnp.where(lane == k, pos.astype(I32), po)
        go = jnp.where(lane == k, es[k] / tot, go)
    pos_ref[...] = po
    gate_ref[...] = go
    cnt_ref[0] = jnp.broadcast_to(c_i, (8, 128))


def _router(hf, rw, rb, tm):
    M = hf.shape[0]
    nt = M // tm
    return pl.pallas_call(
        functools.partial(_router_body, tm=tm),
        grid=(nt,),
        in_specs=[pl.BlockSpec((tm, D_MODEL), lambda i: (i, 0)),
                  pl.BlockSpec((D_MODEL, 128), lambda i: (0, 0)),
                  pl.BlockSpec((1, 128), lambda i: (0, 0))],
        out_specs=[pl.BlockSpec((tm, 128), lambda i: (i, 0)), pl.BlockSpec((tm, 128), lambda i: (i, 0)),
                   pl.BlockSpec((1, 8, 128), lambda i: (i, 0, 0))],
        out_shape=[jax.ShapeDtypeStruct((M, 128), I32), jax.ShapeDtypeStruct((M, 128), F32),
                   jax.ShapeDtypeStruct((nt, 8, 128), I32)],
        compiler_params=_cparams(("parallel",)),
        name="router",
    )(hf, rw, rb)


def _segment_copies(tile, n8_ref, loff_ref, gdst_ref, vm_ref, hbm_ref, sem, nbits, to_hbm, start):
    def body(e, carry):
        j = tile * N_EXPERTS + e
        n8, lo, gd = n8_ref[j], loff_ref[j], gdst_ref[j]
        for b in range(nbits):
            size = SEG_ALIGN << b
            off = (n8 & ((1 << b) - 1)) * SEG_ALIGN

            @pl.when(((n8 >> b) & 1) == 1)
            def _():
                v = vm_ref.at[pl.ds(pl.multiple_of(lo + off, SEG_ALIGN), size)]
                h = hbm_ref.at[pl.ds(pl.multiple_of(gd + off, SEG_ALIGN), size)]
                cp = pltpu.make_async_copy(v, h, sem) if to_hbm else pltpu.make_async_copy(h, v, sem)
                if start:
                    cp.start()
                else:
                    cp.wait()
        return carry

    lax.fori_loop(0, N_EXPERTS, body, 0)


def _dispatch_body(n8_ref, loff_ref, gdst_ref, pos_ref, h_ref, xinit_ref, xbuf_ref, xs_ref, sem, *, tm, rows, nbits):
    del xinit_ref
    tile = pl.program_id(0)
    pos_t = pos_ref[...].astype(F32).T
    sub = lax.broadcasted_iota(I32, (rows, tm), 0).astype(F32)
    perm = jnp.zeros((rows, tm), F32)
    for k in range(TOP_K):
        perm = perm + jnp.where(sub == pos_t[k:k + 1, :], 1.0, 0.0)
    xs_ref[...] = jnp.dot(perm.astype(BF16), h_ref[...], preferred_element_type=F32)
    _segment_copies(tile, n8_ref, loff_ref, gdst_ref, xs_ref, xbuf_ref, sem, nbits, True, True)
    _segment_copies(tile, n8_ref, loff_ref, gdst_ref, xs_ref, xbuf_ref, sem, nbits, True, False)


def _combine_body(n8_ref, loff_ref, gdst_ref, pos_ref, gate_ref, x1_ref, ybuf_ref, out_ref, ys_ref, sem,
                  *, tm, rows, nbits):
    tile = pl.program_id(0)

    @pl.when(tile == 0)
    def _():
        ys_ref[...] = jnp.zeros_like(ys_ref)

    _segment_copies(tile, n8_ref, loff_ref, gdst_ref, ys_ref, ybuf_ref, sem, nbits, False, True)
    lane = lax.broadcasted_iota(I32, (tm, rows), 1)
    pos = pos_ref[...]
    gate = gate_ref[...]
    g = jnp.zeros((tm, rows), F32)
    for k in range(TOP_K):
        g = g + jnp.where(lane == pos[:, k:k + 1], gate[:, k:k + 1], 0.0)
    g_hi = g.astype(BF16)
    g_lo = (g - g_hi.astype(F32)).astype(BF16)
    _segment_copies(tile, n8_ref, loff_ref, gdst_ref, ys_ref, ybuf_ref, sem, nbits, False, False)
    ys = ys_ref[...]
    y_hi = ys.astype(BF16)
    y_lo = (ys - y_hi.astype(F32)).astype(BF16)
    acc = jnp.dot(g_hi, y_hi, preferred_element_type=F32)
    acc = acc + jnp.dot(g_hi, y_lo, preferred_element_type=F32)
    acc = acc + jnp.dot(g_lo, y_hi, preferred_element_type=F32)
    out_ref[...] = x1_ref[...] + acc


def _expert_body(be_ref, nu_ref, x_ref, wgu_ref, bgu_ref, wdn_ref, bdn_ref, y_ref):
    @pl.when(pl.program_id(0) < nu_ref[0])
    def _():
        hgu = jnp.dot(x_ref[...].astype(BF16), wgu_ref[0], preferred_element_type=F32) + bgu_ref[0]
        gate = jnp.minimum(hgu[:, :D_FF], SWIGLU_LIMIT)
        up = jnp.clip(hgu[:, D_FF:], -SWIGLU_LIMIT, SWIGLU_LIMIT)
        act = (up + 1.0) * gate * _sigmoid(SWIGLU_ALPHA * gate)
        y_ref[...] = jnp.dot(act.astype(BF16), wdn_ref[0], preferred_element_type=F32) + bdn_ref[0]

    @pl.when(pl.program_id(0) >= nu_ref[0])
    def _():
        y_ref[...] = jnp.zeros_like(y_ref)


def _experts(xbuf, blk_e, n_used, wgu, bgu, wdn, bdn):
    nb = blk_e.shape[0]
    rb = xbuf.shape[0] // nb
    last = lambda i, nu: jnp.minimum(i, jnp.maximum(nu[0] - 1, 0))
    grid_spec = pltpu.PrefetchScalarGridSpec(
        num_scalar_prefetch=2,
        grid=(nb,),
        in_specs=[pl.BlockSpec((rb, D_MODEL), lambda i, be, nu: (last(i, nu), 0)),
                  pl.BlockSpec((1, D_MODEL, 2 * D_FF), lambda i, be, nu: (be[i], 0, 0)),
                  pl.BlockSpec((1, 1, 2 * D_FF), lambda i, be, nu: (be[i], 0, 0)),
                  pl.BlockSpec((1, D_FF, D_MODEL), lambda i, be, nu: (be[i], 0, 0)),
                  pl.BlockSpec((1, 1, D_MODEL), lambda i, be, nu: (be[i], 0, 0))],
        out_specs=pl.BlockSpec((rb, D_MODEL), lambda i, be, nu: (i, 0)),
    )
    return pl.pallas_call(
        _expert_body,
        grid_spec=grid_spec,
        out_shape=jax.ShapeDtypeStruct((nb * rb, D_MODEL), F32),
        compiler_params=_cparams(("arbitrary",)),
        name="experts",
    )(blk_e, n_used, xbuf, wgu, bgu.reshape(N_EXPERTS, 1, 2 * D_FF), wdn, bdn.reshape(N_EXPERTS, 1, D_MODEL))


def _moe(hf, x1, W):
    M = hf.shape[0]
    tm = min(MOE_TM, M)
    nt = M // tm
    assert M % tm == 0 and tm * TOP_K // SEG_ALIGN <= 256
    pos, gates, cnt = _router(hf, W['router_w'], W['router_b'], tm)

    rb = MOE_RB if M * TOP_K >= N_EXPERTS * MOE_RB else 128
    cpad = (cnt[:, 0, :N_EXPERTS] + (SEG_ALIGN - 1)) // SEG_ALIGN * SEG_ALIGN
    loff = jnp.cumsum(cpad, axis=1) - cpad
    region = (jnp.sum(cpad, axis=0) + rb - 1) // rb * rb
    pend = jnp.cumsum(region)
    gdst = (pend - region)[None, :] + jnp.cumsum(cpad, axis=0) - cpad
    max_rows = M * TOP_K + nt * N_EXPERTS * (SEG_ALIGN - 1)
    nb = (max_rows + rb - 1) // rb + N_EXPERTS
    blk_e = jnp.minimum(jnp.searchsorted(pend, jnp.arange(nb, dtype=I32) * rb, side='right'),
                        N_EXPERTS - 1).astype(I32)
    n_used = (pend[-1] // rb).astype(I32).reshape(1)
    plan = ((cpad // SEG_ALIGN).reshape(-1).astype(I32), loff.reshape(-1).astype(I32),
            gdst.reshape(-1).astype(I32))

    rows = (tm * TOP_K + N_EXPERTS * (SEG_ALIGN - 1) + 127) // 128 * 128
    nbits = (tm * TOP_K // SEG_ALIGN).bit_length()
    any_spec = pl.BlockSpec(memory_space=pl.ANY)
    row = lambda w: pl.BlockSpec((tm, w), lambda i, *_: (i, 0))
    xbuf = pl.pallas_call(
        functools.partial(_dispatch_body, tm=tm, rows=rows, nbits=nbits),
        grid_spec=pltpu.PrefetchScalarGridSpec(
            num_scalar_prefetch=3, grid=(nt,),
            in_specs=[row(128), row(D_MODEL), any_spec],
            out_specs=any_spec,
            scratch_shapes=[pltpu.VMEM((rows, D_MODEL), F32), pltpu.SemaphoreType.DMA(())]),
        out_shape=jax.ShapeDtypeStruct((nb * rb, D_MODEL), F32),
        input_output_aliases={5: 0},
        compiler_params=_cparams(("arbitrary",)),
        name="moe_dispatch",
    )(*plan, pos, hf, jnp.zeros((nb * rb, D_MODEL), F32))
    ybuf = _experts(xbuf, blk_e, n_used, W['exp_w_gu'], W['exp_b_gu'], W['exp_w_down'], W['exp_b_down'])
    return pl.pallas_call(
        functools.partial(_combine_body, tm=tm, rows=rows, nbits=nbits),
        grid_spec=pltpu.PrefetchScalarGridSpec(
            num_scalar_prefetch=3, grid=(nt,),
            in_specs=[row(128), row(128), row(D_MODEL), any_spec],
            out_specs=row(D_MODEL),
            scratch_shapes=[pltpu.VMEM((rows, D_MODEL), F32), pltpu.SemaphoreType.DMA(())]),
        out_shape=jax.ShapeDtypeStruct((M, D_MODEL), F32),
        compiler_params=_cparams(("arbitrary",)),
        name="moe_combine",
    )(*plan, pos, gates, x1, ybuf)


_NT = (((1,), (1,)), ((), ()))


def _dsa_step_body(pt_ref, qi_ref, wi_ref, kin_ref, qbd_ref, kn_ref, vn_ref, tbl_ref, bnew_ref, *refs, g, n_pages):
    del pt_ref
    ki_refs, k_refs, v_refs = refs[:g], refs[g:2 * g], refs[2 * g:3 * g]
    out_ref, s_ref, thr_ref, knew_ref, m_ref, l_ref, acc_ref = refs[3 * g:]
    j = pl.program_id(1)
    ns = n_pages // g
    qi = qi_ref[0]
    wi = wi_ref[0]

    @pl.when(j < ns)
    def _():
        for t in range(g):
            s = lax.dot_general(qi, ki_refs[t][0].astype(BF16), _NT, preferred_element_type=F32)
            row = jnp.sum(jnp.maximum(s, 0.0) * wi, axis=0, keepdims=True)
            s_ref[pl.ds(j * g + t, 1), :] = _sortable_key(row)

    @pl.when(j == ns - 1)
    def _():
        sn = jnp.sum(qi.astype(F32) * kin_ref[0].astype(BF16).astype(F32), axis=-1, keepdims=True)
        knew = _sortable_key(jnp.sum(jnp.maximum(sn, 0.0) * wi, axis=0, keepdims=True))

        def total(x):
            return jnp.sum(jnp.sum(x, axis=0, keepdims=True), axis=1, keepdims=True)

        def count_ge(cand):
            return total((s_ref[...] >= cand).astype(I32)) + (knew >= cand).astype(I32)

        base, cnt = _topk_threshold(count_ge, 1)
        thr_ref[...] = jnp.maximum(base, jnp.int32(INT_MIN + 1))
        knew_ref[...] = knew
        tie = jnp.logical_and(base > INT_MIN, cnt > TOPK)

        @pl.when(jnp.max(tie.astype(I32)) > 0)
        def _():
            need = (TOPK - count_ge(base + 1)).astype(F32)
            keys = s_ref[...]
            eq = (keys == base).astype(F32)
            incl = (lax.broadcasted_iota(I32, (128, 128), 0) <= lax.broadcasted_iota(I32, (128, 128), 1))
            lane_rank = jnp.dot(eq.astype(BF16), incl.astype(BF16), preferred_element_type=F32)
            low = (lax.broadcasted_iota(I32, (n_pages, n_pages), 0) > lax.broadcasted_iota(I32, (n_pages, n_pages), 1))
            row_tot = jnp.broadcast_to(jnp.sum(eq, axis=1, keepdims=True), eq.shape)
            row_off = jnp.dot(low.astype(BF16), row_tot.astype(BF16), preferred_element_type=F32)
            s_ref[...] = jnp.where(eq * (lane_rank + row_off - need) > 0.0, base - 1, keys)
            new_demoted = jnp.logical_and(knew == base, total(eq) + 1.0 > need)
            knew_ref[...] = jnp.where(new_demoted, base - 1, knew)

        m_ref[...] = jnp.full(m_ref.shape, NEG, F32)
        l_ref[...] = jnp.zeros_like(l_ref)
        acc_ref[...] = jnp.zeros_like(acc_ref)

    @pl.when(j >= ns)
    def _():
        thr = thr_ref[...]
        qbd = qbd_ref[0]
        p0 = (j - ns) * g
        blocks = []
        for t in range(g):
            s = lax.dot_general(qbd, k_refs[t][0].astype(BF16), _NT, preferred_element_type=F32)
            if t == g - 1:
                s = s + jnp.where(j == 2 * ns - 1, tbl_ref[...], 0.0)
            blocks.append(jnp.where(s_ref[pl.ds(p0 + t, 1), :] >= thr, s, NEG))
        s_all = jnp.concatenate(blocks, axis=-1)
        m_old = m_ref[...]
        m_new = jnp.maximum(m_old, jnp.max(s_all, axis=-1, keepdims=True))
        alpha = jnp.exp2(m_old - m_new)
        pr = jnp.exp2(s_all - m_new)
        l_ref[...] = alpha * l_ref[...] + jnp.sum(pr, axis=-1, keepdims=True)
        acc = alpha * acc_ref[...]
        for t in range(g):
            acc = acc + jnp.dot(pr[:, t * PAGE_SIZE:(t + 1) * PAGE_SIZE].astype(BF16), v_refs[t][0].astype(BF16),
                                preferred_element_type=F32)
        acc_ref[...] = acc
        m_ref[...] = m_new

    @pl.when(j == 2 * ns - 1)
    def _():
        kn = kn_ref[0].astype(BF16).astype(F32)
        s_new = jnp.sum(qbd_ref[0].astype(F32) * kn, axis=-1, keepdims=True) + bnew_ref[...]
        s_new = jnp.where(knew_ref[...] >= thr_ref[...], s_new, NEG)
        m_old = m_ref[...]
        m_new = jnp.maximum(m_old, s_new)
        alpha = jnp.exp2(m_old - m_new)
        pn = jnp.exp2(s_new - m_new)
        l = alpha * l_ref[...] + pn
        o = (alpha * acc_ref[...] + pn * vn_ref[0].astype(BF16).astype(F32)) / l
        kvh = lax.broadcasted_iota(I32, (N_HEADS, HEAD_DIM), 0) // (N_HEADS // N_KV_HEADS)
        res = jnp.zeros((N_HEADS, HEAD_DIM), F32)
        for c in range(N_KV_HEADS):
            res = jnp.where(kvh == c, o[:, c * HEAD_DIM:(c + 1) * HEAD_DIM], res)
        out_ref[0] = res.astype(out_ref.dtype)


def _dsa_step(q, k_new, v_new, qi, wi, ki_new, cache_k, cache_v, cache_kidx, page_table, rel_bias):
    Bd, n_pages = page_table.shape
    past = n_pages * PAGE_SIZE
    g = min(DEC_G, n_pages)
    assert n_pages % g == 0 and PAGE_SIZE >= _FAR_DIST and TOPK <= (past + 1) // 4
    ns = n_pages // g
    n_phys = cache_k.shape[0]
    kvw = N_KV_HEADS * HEAD_DIM
    rel = (rel_bias - rel_bias[N_BUCKETS - 1]).astype(F32) * LOG2E
    tbl = rel[_t5_bucket(PAGE_SIZE - jnp.arange(PAGE_SIZE, dtype=I32))].T
    bnew = rel[_t5_bucket(jnp.zeros((1,), I32))].T
    onehot = (jnp.arange(N_HEADS)[:, None] // (N_HEADS // N_KV_HEADS) == jnp.arange(N_KV_HEADS)[None, :])
    qbd = (q.reshape(Bd, N_HEADS, 1, HEAD_DIM) * onehot.astype(q.dtype)[None, :, :, None]).reshape(Bd, N_HEADS, kvw)

    per_seq = lambda shape: pl.BlockSpec((1,) + shape, lambda b, j, pt: (b, 0, 0))
    full = lambda shape: pl.BlockSpec(shape, lambda b, j, pt: (0, 0))

    def ki_page(t):
        return pl.BlockSpec((1, PAGE_SIZE, IDX_DIM),
                            lambda b, j, pt: (pt[b * n_pages + jnp.where(j < ns, j * g + t, (ns - 1) * g + t)], 0, 0))

    def kv_page(t):
        return pl.BlockSpec((1, PAGE_SIZE, kvw),
                            lambda b, j, pt: (pt[b * n_pages + jnp.where(j < ns, t, (j - ns) * g + t)], 0, 0))

    grid_spec = pltpu.PrefetchScalarGridSpec(
        num_scalar_prefetch=1, grid=(Bd, 2 * ns),
        in_specs=[per_seq((N_IDX_HEADS, IDX_DIM)), per_seq((N_IDX_HEADS, 1)), per_seq((1, IDX_DIM)),
                  per_seq((N_HEADS, kvw)), per_seq((1, kvw)), per_seq((1, kvw)),
                  full((N_HEADS, PAGE_SIZE)), full((N_HEADS, 1))]
                 + [ki_page(t) for t in range(g)] + [kv_page(t) for t in range(g)] + [kv_page(t) for t in range(g)],
        out_specs=per_seq((N_HEADS, HEAD_DIM)),
        scratch_shapes=[pltpu.VMEM((n_pages, PAGE_SIZE), I32), pltpu.VMEM((1, 1), I32), pltpu.VMEM((1, 1), I32),
                        pltpu.VMEM((N_HEADS, 1), F32), pltpu.VMEM((N_HEADS, 1), F32),
                        pltpu.VMEM((N_HEADS, kvw), F32)])
    ck = cache_k.reshape(n_phys, PAGE_SIZE, kvw)
    cv = cache_v.reshape(n_phys, PAGE_SIZE, kvw)
    out = pl.pallas_call(
        functools.partial(_dsa_step_body, g=g, n_pages=n_pages),
        grid_spec=grid_spec,
        out_shape=jax.ShapeDtypeStruct((Bd, N_HEADS, HEAD_DIM), BF16),
        compiler_params=_cparams(("arbitrary", "arbitrary")),
        name="dsa_step",
    )(page_table.reshape(-1), qi.reshape(Bd, N_IDX_HEADS, IDX_DIM), wi.reshape(Bd, N_IDX_HEADS, 1),
      ki_new.reshape(Bd, 1, IDX_DIM), qbd, k_new.reshape(Bd, 1, kvw), v_new.reshape(Bd, 1, kvw), tbl, bnew,
      *([cache_kidx] * g), *([ck] * g), *([cv] * g))
    return out.reshape(Bd, N_HEADS * HEAD_DIM)


def _mem_step_body(xq_ref, mk_ref, mv_ref, sel_ref, out_ref):
    s = lax.dot_general(xq_ref[0], mk_ref[0].astype(BF16), _NT, preferred_element_type=F32) * X_HEAD_DIM ** -0.5
    p = jnp.exp(s - jnp.max(s, axis=-1, keepdims=True))
    p = p / jnp.sum(p, axis=-1, keepdims=True)
    r = jnp.dot(p.astype(BF16), mv_ref[0].astype(BF16), preferred_element_type=F32)
    out_ref[0] = jnp.sum(r * sel_ref[...], axis=0, keepdims=True).astype(out_ref.dtype)


def _mem_step(xq, mem_k, mem_v):
    Bd, M = mem_k.shape[:2]
    w = X_HEADS * X_HEAD_DIM
    eye = jnp.eye(X_HEADS, dtype=xq.dtype)
    xbd = (xq[:, :, None, :] * eye[None, :, :, None]).reshape(Bd, X_HEADS, w)
    xbd = jnp.pad(xbd, ((0, 0), (0, 8 - X_HEADS), (0, 0)))
    sel = jnp.pad(jnp.repeat(jnp.eye(X_HEADS, dtype=F32), X_HEAD_DIM, axis=1), ((0, 8 - X_HEADS), (0, 0)))
    out = pl.pallas_call(
        _mem_step_body,
        grid=(Bd,),
        in_specs=[pl.BlockSpec((1, 8, w), lambda b: (b, 0, 0)), pl.BlockSpec((1, M, w), lambda b: (b, 0, 0)),
                  pl.BlockSpec((1, M, w), lambda b: (b, 0, 0)), pl.BlockSpec((8, w), lambda b: (0, 0))],
        out_specs=pl.BlockSpec((1, 1, w), lambda b: (b, 0, 0)),
        out_shape=jax.ShapeDtypeStruct((Bd, 1, w), BF16),
        compiler_params=_cparams(("parallel",)),
        name="mem_step",
    )(xbd, mem_k.reshape(Bd, M, w), mem_v.reshape(Bd, M, w), sel)
    return out.reshape(Bd, w)


def _pad_heads_cols(w):
    K = w.shape[0]
    w = w.reshape(K, X_HEADS, X_HEAD_DIM)
    return jnp.pad(w, ((0, 0), (0, 0), (0, X_HEAD_PAD - X_HEAD_DIM))).reshape(K, X_HEADS * X_HEAD_PAD)


def _pad_heads_vec(g):
    return jnp.pad(g, (0, X_HEAD_PAD - X_HEAD_DIM))


def _prep_weights(P):
    W = {}
    w_in = P['w_in'].astype(BF16)
    widths = (D_RNN, D_RNN, N_HEADS * HEAD_DIM, N_KV_HEADS * HEAD_DIM, N_KV_HEADS * HEAD_DIM,
              N_IDX_HEADS * IDX_DIM, IDX_DIM, N_IDX_HEADS, X_HEADS * X_HEAD_DIM, 3 * D_MODEL)
    offs = np.cumsum((0,) + widths)
    seg = lambda i: w_in[:, offs[i]:offs[i + 1]]
    W['w_xy'] = w_in[:, offs[0]:offs[2]]
    W['w_q'], W['w_k'], W['w_v'], W['w_qi'] = seg(2), seg(3), seg(4), seg(5)
    W['w_kiwi'] = jnp.pad(w_in[:, offs[6]:offs[8]], ((0, 0), (0, 128 - IDX_DIM - N_IDX_HEADS)))
    W['w_xq'] = _pad_heads_cols(seg(8))
    W['w_g'] = seg(9)
    W['q_gain'] = jnp.tile(P['q_norm'], N_HEADS)
    W['q_gain_log2'] = W['q_gain'] * (HEAD_DIM ** -0.5 * LOG2E)
    W['k_gain'] = jnp.tile(P['k_norm'], N_KV_HEADS)
    W['xq_gain'] = jnp.tile(_pad_heads_vec(P['xq_norm']), X_HEADS)
    W['xk_gain'] = jnp.tile(_pad_heads_vec(P['xk_norm']), X_HEADS)
    wmk, wmv = jnp.split(P['w_mem_kv'].astype(BF16), 2, axis=-1)
    W['w_mem_k'] = _pad_heads_cols(wmk)
    W['w_mem_v'] = _pad_heads_cols(wmv)
    W['wg_lru'] = _block_diag_gates(P['lru_wa'], P['lru_wx'])
    W['w_br_rnn'] = P['w_br_rnn'].astype(BF16)
    W['w_br_attn'] = P['w_br_attn'].astype(BF16)
    W['w_br_mem'] = _pad_heads_cols(P['w_br_mem'].astype(BF16).T).T
    W['w_out'] = P['w_out'].astype(BF16)
    W['router_w'] = jnp.pad(P['router_w'].astype(BF16), ((0, 0), (0, 128 - N_EXPERTS)))
    W['router_b'] = jnp.pad(P['router_b'].astype(F32), (0, 128 - N_EXPERTS), constant_values=-1e30).reshape(1, 128)
    W['exp_w_gu'] = P['exp_w_gu'].astype(BF16)
    W['exp_w_down'] = P['exp_w_down'].astype(BF16)
    for name in ('norm_mix', 'conv_w', 'conv_b', 'lru_ba', 'lru_bx', 'lru_lambda', 'rel_bias', 'mem_norm',
                 'norm_ffn', 'exp_b_gu', 'exp_b_down'):
        W[name] = P[name]
    return W


def _unpad_heads(a):
    return a.reshape(a.shape[:-1] + (X_HEADS, X_HEAD_PAD))[..., :X_HEAD_DIM]


def _in_proj(x2, W, q_gain):
    g = W['norm_mix']
    o = {}
    (o['xy'],) = _norm_proj(x2, g, W['w_xy'], [F32], tn=1024)
    (o['q'],) = _norm_proj(x2, g, W['w_q'], [BF16], tn=512, mode="headnorm", head_gain=q_gain)
    o['k'], o['k16'] = _norm_proj(x2, g, W['w_k'], [F32, BF16], tn=512, mode="headnorm", head_gain=W['k_gain'])
    o['v'], o['v16'] = _norm_proj(x2, g, W['w_v'], [F32, BF16], tn=512)
    (o['qi'],) = _norm_proj(x2, g, W['w_qi'], [BF16], tn=512)
    (o['kiwi'],) = _norm_proj(x2, g, W['w_kiwi'], [F32], tn=128)
    (o['xq'],) = _norm_proj(x2, g, W['w_xq'], [BF16], tn=512, mode="headnorm", head_gain=W['xq_gain'],
                            hd_pad=X_HEAD_PAD, hd_true=X_HEAD_DIM)
    (o['g'],) = _norm_proj(x2, g, W['w_g'], [F32], tn=1024)
    return o


def kernel(x_prompt, x_sample, cache_k, cache_v, cache_kidx, cache_mem_k, cache_mem_v, state_conv, state_rglru, page_table, mem_prompt, norm_mix, w_in, conv_w, conv_b, lru_wa, lru_ba, lru_wx, lru_bx, lru_lambda, q_norm, k_norm, rel_bias, mem_norm, w_mem_kv, xq_norm, xk_norm, w_br_rnn, w_br_attn, w_br_mem, w_out, norm_ffn, router_w, router_b, exp_w_gu, exp_b_gu, exp_w_down, exp_b_down):
    W = _prep_weights(dict(norm_mix=norm_mix, w_in=w_in, conv_w=conv_w, conv_b=conv_b, lru_wa=lru_wa,
                           lru_ba=lru_ba, lru_wx=lru_wx, lru_bx=lru_bx, lru_lambda=lru_lambda, q_norm=q_norm,
                           k_norm=k_norm, rel_bias=rel_bias, mem_norm=mem_norm, w_mem_kv=w_mem_kv,
                           xq_norm=xq_norm, xk_norm=xk_norm, w_br_rnn=w_br_rnn, w_br_attn=w_br_attn,
                           w_br_mem=w_br_mem, w_out=w_out, norm_ffn=norm_ffn, router_w=router_w,
                           router_b=router_b, exp_w_gu=exp_w_gu, exp_b_gu=exp_b_gu, exp_w_down=exp_w_down,
                           exp_b_down=exp_b_down))
    lru = (W['conv_w'], W['conv_b'], W['wg_lru'], W['lru_ba'], W['lru_bx'], W['lru_lambda'])

    B, T, D = x_prompt.shape
    xp2 = x_prompt.reshape(B * T, D)
    memp2 = mem_prompt.reshape(B * N_MEM, D)
    mk_pad, mk16 = _norm_proj(memp2, W['mem_norm'], W['w_mem_k'], [F32, BF16], tn=512, mode="headnorm",
                              head_gain=W['xk_gain'], hd_pad=X_HEAD_PAD, hd_true=X_HEAD_DIM)
    mv_pad, mv16 = _norm_proj(memp2, W['mem_norm'], W['w_mem_v'], [F32, BF16], tn=512)
    mk_p = _unpad_heads(mk_pad).reshape(B, N_MEM, X_HEADS, X_HEAD_DIM)
    mv_p = _unpad_heads(mv_pad).reshape(B, N_MEM, X_HEADS, X_HEAD_DIM)

    o = _in_proj(xp2, W, W['q_gain_log2'])
    xy = o['xy'].reshape(B, T, 2 * D_RNN)
    rnn_out, rg_p = _rglru_prompt(xy, *lru)
    conv_p = xy[:, T - (CONV_W - 1):, :D_RNN]
    ki_p = o['kiwi'][:, :IDX_DIM].reshape(B, T, IDX_DIM)
    wi_p = o['kiwi'][:, IDX_DIM:IDX_DIM + N_IDX_HEADS].reshape(B, T, N_IDX_HEADS)
    qi_h = o['qi'].reshape(B, T, N_IDX_HEADS, IDX_DIM).transpose(0, 2, 1, 3)
    kit = ki_p.astype(BF16).transpose(0, 2, 1)
    attn_out = _dsa_prompt(o['q'].reshape(B, T, -1), o['k16'].reshape(B, T, -1), o['v16'].reshape(B, T, -1),
                           qi_h, wi_p, kit, W['rel_bias'])
    mem_out = _mem_attn(o['xq'].reshape(B, T, -1), mk16.reshape(B, N_MEM, -1), mv16.reshape(B, N_MEM, -1))
    x1, hf = _merge(xp2, rnn_out.reshape(B * T, -1), attn_out.reshape(B * T, -1), mem_out.reshape(B * T, -1),
                    o['g'], W['w_br_rnn'], W['w_br_attn'], W['w_br_mem'], W['w_out'], W['norm_ffn'])
    y_prompt = _moe(hf, x1, W).reshape(B, T, D)
    k_p = o['k'].reshape(B, T, N_KV_HEADS, HEAD_DIM)
    v_p = o['v'].reshape(B, T, N_KV_HEADS, HEAD_DIM)

    Bd, S, _ = x_sample.shape
    xs2 = x_sample.reshape(Bd * S, D)
    os_ = _in_proj(xs2, W, W['q_gain_log2'])
    rnn_s, rg_s = _rglru_step(os_['xy'], state_conv.transpose(1, 0, 2), state_rglru, *lru)
    conv_s = jnp.concatenate([state_conv[:, 1:], os_['xy'][:, None, :D_RNN]], axis=1)
    k_s = os_['k'].reshape(Bd, S, N_KV_HEADS, HEAD_DIM)
    v_s = os_['v'].reshape(Bd, S, N_KV_HEADS, HEAD_DIM)
    ki_s = os_['kiwi'][:, :IDX_DIM].reshape(Bd, S, IDX_DIM)
    wi_s = os_['kiwi'][:, IDX_DIM:IDX_DIM + N_IDX_HEADS].reshape(Bd, S, N_IDX_HEADS)
    assert S == 1
    attn_s = _dsa_step(os_['q'], os_['k'], os_['v'], os_['qi'], wi_s.reshape(Bd, -1),
                       ki_s.reshape(Bd, -1), cache_k, cache_v, cache_kidx, page_table, W['rel_bias'])
    mem_s = _mem_step(_unpad_heads(os_['xq']), cache_mem_k, cache_mem_v)
    mem_s = jnp.pad(mem_s.reshape(Bd, X_HEADS, X_HEAD_DIM),
                    ((0, 0), (0, 0), (0, X_HEAD_PAD - X_HEAD_DIM))).reshape(Bd, -1)
    x1s, hfs = _merge(xs2, rnn_s, attn_s, mem_s, os_['g'],
                      W['w_br_rnn'], W['w_br_attn'], W['w_br_mem'], W['w_out'], W['norm_ffn'])
    y_sample = _moe(hfs, x1s, W).reshape(Bd, S, D)

    return (y_prompt, y_sample, k_p, v_p, ki_p, mk_p, mv_p, conv_p, rg_p.reshape(B, D_RNN),
            k_s, v_s, ki_s, conv_s, rg_s)
```

```python
import functools
import math

import numpy as np
import jax
import jax.numpy as jnp
from jax import lax
from jax.experimental import pallas as pl
from jax.experimental.pallas import tpu as pltpu

F32 = jnp.float32
BF16 = jnp.bfloat16
I32 = jnp.int32

D_MODEL = 1024
D_RNN = 1024
LRU_BLOCKS = 16
LRU_BW = 64
CONV_W = 4
LRU_C = 8.0
N_HEADS = 8
N_KV_HEADS = 4
HEAD_DIM = 128
N_IDX_HEADS = 8
IDX_DIM = 64
TOPK = 256
PAGE_SIZE = 128
N_MEM = 256
X_HEADS = 4
X_HEAD_DIM = 192
X_HEAD_PAD = 256
N_BUCKETS = 32
MAX_DISTANCE = 128
N_EXPERTS = 32
TOP_K = 4
D_FF = 1024
SWIGLU_LIMIT = 7.0
SWIGLU_ALPHA = 1.702
EPS = 1e-6

INT_MIN = -2 ** 31
NEG = -1e30
LOG2E = 1.4426950408889634
VMEM_LIMIT = 56 * 1024 * 1024

DSA_TQ = 256
DSA_KB = 1024
DEC_G = 16
MOE_RB = 256
MOE_TM = 256
SEG_ALIGN = 8


def _cparams(sem):
    return pltpu.CompilerParams(dimension_semantics=sem, vmem_limit_bytes=VMEM_LIMIT)


def _sigmoid(x):
    return 1.0 / (1.0 + jnp.exp(-x))


def _gelu_tanh(x):
    return 0.5 * x * (1.0 + jnp.tanh(0.7978845608028654 * (x + 0.044715 * x * x * x)))


def _head_norm(y, gain, hd_pad, hd_true):
    parts = []
    for s in range(0, y.shape[1], hd_pad):
        ys = y[:, s:s + hd_pad]
        ms = jnp.sum(ys * ys, axis=-1, keepdims=True) * (1.0 / hd_true)
        parts.append(ys * lax.rsqrt(ms + EPS) * gain[:, s:s + hd_pad])
    return parts[0] if len(parts) == 1 else jnp.concatenate(parts, axis=-1)


def _proj_body(x_ref, g_ref, w_ref, *refs, mode, hd_pad, hd_true, n_out):
    if mode == "headnorm":
        gain_ref, refs = refs[0], refs[1:]
    outs, h_ref = refs[:n_out], refs[n_out]

    @pl.when(pl.program_id(1) == 0)
    def _():
        x = x_ref[...]
        ms = jnp.mean(x * x, axis=-1, keepdims=True)
        h_ref[...] = (x * lax.rsqrt(ms + EPS) * g_ref[...]).astype(BF16)

    y = jnp.dot(h_ref[...], w_ref[...], preferred_element_type=F32)
    if mode == "headnorm":
        y = _head_norm(y, gain_ref[...], hd_pad, hd_true)
    for o in outs:
        o[...] = y.astype(o.dtype)


def _norm_proj(x, gain, w, out_dtypes, *, tn, mode="plain", head_gain=None, hd_pad=128, hd_true=128):
    M, K = x.shape
    N = w.shape[1]
    tm = min(512, M)
    assert M % tm == 0 and N % tn == 0
    in_specs = [pl.BlockSpec((tm, K), lambda i, j: (i, 0)),
                pl.BlockSpec((1, K), lambda i, j: (0, 0)),
                pl.BlockSpec((K, tn), lambda i, j: (0, j))]
    args = [x, gain.reshape(1, K), w]
    if mode == "headnorm":
        in_specs.append(pl.BlockSpec((1, tn), lambda i, j: (0, j)))
        args.append(head_gain.reshape(1, N))
    body = functools.partial(_proj_body, mode=mode, hd_pad=hd_pad, hd_true=hd_true, n_out=len(out_dtypes))
    return pl.pallas_call(
        body,
        grid=(M // tm, N // tn),
        in_specs=in_specs,
        out_specs=[pl.BlockSpec((tm, tn), lambda i, j: (i, j)) for _ in out_dtypes],
        out_shape=[jax.ShapeDtypeStruct((M, N), dt) for dt in out_dtypes],
        scratch_shapes=[pltpu.VMEM((tm, K), BF16)],
        compiler_params=_cparams(("parallel", "arbitrary")),
        name="norm_proj_" + mode,
    )(*args)


def _softplus(z):
    return jnp.maximum(z, 0.0) + jnp.log1p(jnp.exp(-jnp.abs(z)))


def _lru_terms(xc, wg_ref, ba, bx, lam, a_out, b_out):
    sp = _softplus(-lam)
    for gi in range(4):
        sl = slice(gi * 256, (gi + 1) * 256)
        xg = xc[:, sl]
        z = jnp.dot(xg.astype(BF16), wg_ref[gi], preferred_element_type=F32)
        r = _sigmoid(z[:, :256] + ba[:, sl])
        ig = _sigmoid(z[:, 256:] + bx[:, sl])
        log_a = -LRU_C * r * sp[:, sl]
        a_out[:, sl] = jnp.exp(log_a)
        b_out[:, sl] = jnp.sqrt(1.0 - jnp.exp(2.0 * log_a)) * ig * xg


def _rglru_prompt_body(xr_ref, yr_ref, cw_ref, cb_ref, wg_ref, ba_ref, bx_ref, lam_ref,
                       out_ref, hl_ref, xp_ref, a_ref, b_ref, h_ref, hc_ref, *, tc):
    t = pl.program_id(1)

    @pl.when(t == 0)
    def _():
        xp_ref[0:8, :] = jnp.zeros((8, D_RNN), F32)
        hc_ref[...] = jnp.zeros_like(hc_ref)

    @pl.when(t > 0)
    def _():
        xp_ref[0:8, :] = xp_ref[tc:tc + 8, :]

    xp_ref[8:8 + tc, :] = xr_ref[0]
    cw = cw_ref[...]
    xc = cb_ref[...] + cw[0:1] * xp_ref[5:5 + tc, :]
    xc = xc + cw[1:2] * xp_ref[6:6 + tc, :]
    xc = xc + cw[2:3] * xp_ref[7:7 + tc, :]
    xc = xc + cw[3:4] * xp_ref[8:8 + tc, :]
    _lru_terms(xc, wg_ref, ba_ref[...], bx_ref[...], lam_ref[...], a_ref, b_ref)

    def step(i, h):
        h = a_ref[pl.ds(i, 1), :] * h + b_ref[pl.ds(i, 1), :]
        h_ref[pl.ds(i, 1), :] = h
        return h

    h = lax.fori_loop(0, tc, step, hc_ref[0:1, :], unroll=8)
    hc_ref[0:1, :] = h
    out_ref[0] = (h_ref[...] * _gelu_tanh(yr_ref[0])).astype(out_ref.dtype)

    @pl.when(t == pl.num_programs(1) - 1)
    def _():
        hl_ref[0] = h


def _rglru_prompt(xy, conv_w, conv_b, wg, ba, bx, lam):
    B, T, _ = xy.shape
    tc = min(256, T)
    vec = lambda a: a.reshape(1, D_RNN)
    full = lambda shape: pl.BlockSpec(shape, lambda b, t: (0,) * len(shape))
    return pl.pallas_call(
        functools.partial(_rglru_prompt_body, tc=tc),
        grid=(B, T // tc),
        in_specs=[pl.BlockSpec((1, tc, D_RNN), lambda b, t: (b, t, 0)),
                  pl.BlockSpec((1, tc, D_RNN), lambda b, t: (b, t, 1)),
                  full((CONV_W, D_RNN)), full((1, D_RNN)), full((4, 256, 512)),
                  full((1, D_RNN)), full((1, D_RNN)), full((1, D_RNN))],
        out_specs=[pl.BlockSpec((1, tc, D_RNN), lambda b, t: (b, t, 0)),
                   pl.BlockSpec((1, 1, D_RNN), lambda b, t: (b, 0, 0))],
        out_shape=[jax.ShapeDtypeStruct((B, T, D_RNN), BF16),
                   jax.ShapeDtypeStruct((B, 1, D_RNN), F32)],
        scratch_shapes=[pltpu.VMEM((tc + 8, D_RNN), F32), pltpu.VMEM((tc, D_RNN), F32),
                        pltpu.VMEM((tc, D_RNN), F32), pltpu.VMEM((tc, D_RNN), F32),
                        pltpu.VMEM((8, D_RNN), F32)],
        compiler_params=_cparams(("parallel", "arbitrary")),
        name="rglru_prompt",
    )(xy, xy, conv_w, vec(conv_b), wg, vec(ba), vec(bx), vec(lam))


def _rglru_step_body(xr_ref, yr_ref, prev_ref, h0_ref, cw_ref, cb_ref, wg_ref, ba_ref, bx_ref, lam_ref,
                     out_ref, hn_ref, a_ref, b_ref):
    cw = cw_ref[...]
    xc = cb_ref[...] + cw[0:1] * prev_ref[0] + cw[1:2] * prev_ref[1] + cw[2:3] * prev_ref[2] + cw[3:4] * xr_ref[...]
    _lru_terms(xc, wg_ref, ba_ref[...], bx_ref[...], lam_ref[...], a_ref, b_ref)
    h = a_ref[...] * h0_ref[...] + b_ref[...]
    hn_ref[...] = h
    out_ref[...] = (h * _gelu_tanh(yr_ref[...])).astype(out_ref.dtype)


def _rglru_step(xy, prev, h0, conv_w, conv_b, wg, ba, bx, lam):
    R = xy.shape[0]
    vec = lambda a: a.reshape(1, D_RNN)
    full = lambda shape: pl.BlockSpec(shape, lambda i: (0,) * len(shape))
    return pl.pallas_call(
        _rglru_step_body,
        grid=(1,),
        in_specs=[pl.BlockSpec((R, D_RNN), lambda i: (0, 0)), pl.BlockSpec((R, D_RNN), lambda i: (0, 1)),
                  full((CONV_W - 1, R, D_RNN)), full((R, D_RNN)),
                  full((CONV_W, D_RNN)), full((1, D_RNN)), full((4, 256, 512)),
                  full((1, D_RNN)), full((1, D_RNN)), full((1, D_RNN))],
        out_specs=[full((R, D_RNN)), full((R, D_RNN))],
        out_shape=[jax.ShapeDtypeStruct((R, D_RNN), BF16), jax.ShapeDtypeStruct((R, D_RNN), F32)],
        scratch_shapes=[pltpu.VMEM((R, D_RNN), F32), pltpu.VMEM((R, D_RNN), F32)],
        compiler_params=_cparams(("arbitrary",)),
        name="rglru_step",
    )(xy, xy, prev, h0, conv_w, vec(conv_b), wg, vec(ba), vec(bx), vec(lam))


def _block_diag_gates(wa, wx):
    def bd(w):
        w4 = w.reshape(4, 4, LRU_BW, LRU_BW)
        return jnp.einsum('gaij,ab->gaibj', w4, jnp.eye(4, dtype=w.dtype)).reshape(4, 256, 256)
    return jnp.concatenate([bd(wa), bd(wx)], axis=-1).astype(BF16)


def _sortable_key(s):
    bits = pltpu.bitcast(s + 0.0, I32)
    return jnp.where(bits >= 0, bits, bits ^ jnp.int32(0x7FFFFFFF))


def _topk_threshold(count_ge, rows):
    cnt0 = count_ge(jnp.zeros((rows, 1), I32))
    pos = cnt0 >= TOPK
    base = jnp.where(pos, jnp.int32(0), jnp.int32(INT_MIN))
    cnt = jnp.where(pos, cnt0, jnp.int32(TOPK))

    def bit_body(i, carry):
        base, cnt = carry
        cand = base | jnp.left_shift(jnp.int32(1), jnp.int32(30) - i)
        c = count_ge(cand)
        ok = c >= TOPK
        return jnp.where(ok, cand, base), jnp.where(ok, c, cnt)

    return lax.fori_loop(0, 31, bit_body, (base, cnt))


def _attend_block(q_ref, k, v, mask, bias_of_head, m_ref, l_ref, acc_ref):
    for h in range(N_HEADS):
        c = h // (N_HEADS // N_KV_HEADS)
        hs = slice(h * HEAD_DIM, (h + 1) * HEAD_DIM)
        cs = slice(c * HEAD_DIM, (c + 1) * HEAD_DIM)
        s = lax.dot_general(q_ref[0, :, hs], k[:, cs], (((1,), (1,)), ((), ())),
                            preferred_element_type=F32)
        bias = bias_of_head(h)
        s = jnp.where(mask, s if bias is None else s + bias, NEG)
        m_old = m_ref[h]
        m_new = jnp.maximum(m_old, jnp.max(s, axis=-1, keepdims=True))
        alpha = jnp.exp2(m_old - m_new)
        p = jnp.exp2(s - m_new)
        l_ref[h] = alpha * l_ref[h] + jnp.sum(p, axis=-1, keepdims=True)
        acc_ref[:, hs] = alpha * acc_ref[:, hs] + jnp.dot(p.astype(BF16), v[:, cs],
                                                          preferred_element_type=F32)
        m_ref[h] = m_new


def _dsa_prompt_body(qi_ref, wi_ref, kit_ref, q_ref, kf_ref, vf_ref, kp_ref, vp_ref,
                     kd_ref, vd_ref, tbd_ref, tbp_ref, out_ref,
                     s_ref, thr_ref, m_ref, l_ref, acc_ref, *, tq, kb):
    qb = pl.program_id(1)
    st = pl.program_id(2)
    t0 = qb * tq
    far_len = jnp.maximum(qb - 1, 0) * tq
    n_far = (far_len + kb - 1) // kb

    @pl.when(st == 0)
    def _():
        m_ref[...] = jnp.full(m_ref.shape, NEG, F32)
        l_ref[...] = jnp.zeros_like(l_ref)
        acc_ref[...] = jnp.zeros_like(acc_ref)

        nkc = (t0 + tq + kb - 1) // kb
        wi = wi_ref[0]
        row = t0 + lax.broadcasted_iota(I32, (tq, kb), 0)
        lane = lax.broadcasted_iota(I32, (tq, kb), 1)

        def score_chunk(c, carry):
            off = pl.multiple_of(c * kb, kb)
            kic = kit_ref[0, :, pl.ds(off, kb)]
            acc = jnp.zeros((tq, kb), F32)
            for h in range(N_IDX_HEADS):
                s = jnp.dot(qi_ref[0, h], kic, preferred_element_type=F32)
                acc = acc + jnp.maximum(s, 0.0) * wi[:, h:h + 1]
            key = jnp.where(off + lane <= row, _sortable_key(acc), jnp.int32(INT_MIN))
            s_ref[:, pl.ds(off, kb)] = key
            return carry

        lax.fori_loop(0, nkc, score_chunk, 0)

        def count_ge(cand):
            def body(c, part):
                off = pl.multiple_of(c * kb, kb)
                ge = (s_ref[:, pl.ds(off, kb)] >= cand).astype(I32)
                for j in range(kb // 128):
                    part = part + ge[:, j * 128:(j + 1) * 128]
                return part
            part = lax.fori_loop(0, nkc, body, jnp.zeros((tq, 128), I32))
            return jnp.sum(part, axis=-1, keepdims=True)

        base, cnt_base = _topk_threshold(count_ge, tq)
        thr = jnp.maximum(base, jnp.int32(INT_MIN + 1))
        thr_ref[...] = thr

        tie = jnp.logical_and(base > INT_MIN, cnt_base > TOPK)

        @pl.when(jnp.max(tie.astype(I32)) > 0)
        def _():
            n_gt = count_ge(base + 1)
            need = jnp.where(tie, TOPK - n_gt, jnp.int32(2 ** 30)).astype(F32)
            sub = 256
            incl = (lax.broadcasted_iota(I32, (sub, sub), 0) <= lax.broadcasted_iota(I32, (sub, sub), 1)).astype(BF16)

            def tie_chunk(c, seen):
                for j in range(kb // sub):
                    off = pl.multiple_of(c * kb + j * sub, sub)
                    blk = s_ref[:, pl.ds(off, sub)]
                    eq = (blk == base).astype(F32)
                    rank = jnp.dot(eq.astype(BF16), incl, preferred_element_type=F32) + seen
                    demote = eq * (rank - need) > 0.0
                    s_ref[:, pl.ds(off, sub)] = jnp.where(demote, base - 1, blk)
                    seen = seen + jnp.sum(eq, axis=-1, keepdims=True)
                return seen

            lax.fori_loop(0, nkc, tie_chunk, jnp.zeros((tq, 1), F32))

        @pl.when(qb > 0)
        def _():
            off = pl.multiple_of(t0 - tq, tq)
            mask = s_ref[:, pl.ds(off, tq)] >= thr
            _attend_block(q_ref, kp_ref[0], vp_ref[0], mask, lambda h: tbp_ref[h], m_ref, l_ref, acc_ref)

        off = pl.multiple_of(t0, tq)
        mask = s_ref[:, pl.ds(off, tq)] >= thr
        _attend_block(q_ref, kd_ref[0], vd_ref[0], mask, lambda h: tbd_ref[h], m_ref, l_ref, acc_ref)

    @pl.when(jnp.logical_and(st > 0, st - 1 < n_far))
    def _():
        off = pl.multiple_of((st - 1) * kb, kb)
        col = off + lax.broadcasted_iota(I32, (tq, kb), 1)
        keys = jnp.where(col < far_len, s_ref[:, pl.ds(off, kb)], jnp.int32(INT_MIN))
        _attend_block(q_ref, kf_ref[0], vf_ref[0], keys >= thr_ref[...], lambda h: None, m_ref, l_ref, acc_ref)

    @pl.when(st == pl.num_programs(2) - 1)
    def _():
        for h in range(N_HEADS):
            hs = slice(h * HEAD_DIM, (h + 1) * HEAD_DIM)
            out_ref[0, :, hs] = (acc_ref[:, hs] / l_ref[h]).astype(out_ref.dtype)


def _t5_bucket(dist):
    n = jnp.maximum(dist, 0)
    max_exact = N_BUCKETS // 2
    nf = jnp.maximum(n, 1).astype(F32)
    large = max_exact + (jnp.log(nf / max_exact) / math.log(MAX_DISTANCE / max_exact)
                         * (N_BUCKETS - max_exact)).astype(I32)
    large = jnp.minimum(large, N_BUCKETS - 1)
    return jnp.where(n < max_exact, n, large)


_FAR_DIST = int(math.ceil((N_BUCKETS // 2) * (MAX_DISTANCE / (N_BUCKETS // 2)) **
                          ((N_BUCKETS - 1 - N_BUCKETS // 2) / (N_BUCKETS - N_BUCKETS // 2)))) + 1


def _dsa_prompt(q, k, v, qi_h, wi, kit, rel_bias):
    B, T, _ = q.shape
    tq = min(DSA_TQ, T)
    kb = min(DSA_KB, T)
    assert T % tq == 0 and T % kb == 0 and kb % tq == 0 and tq >= _FAR_DIST and T >= 4 * TOPK
    nq = T // tq
    n_far_max = max(1, ((nq - 2) * tq + kb - 1) // kb) if nq > 1 else 1
    d = jnp.arange(tq, dtype=I32)[:, None] - jnp.arange(tq, dtype=I32)[None, :]
    rel = (rel_bias - rel_bias[N_BUCKETS - 1]).astype(F32) * LOG2E
    tb_diag = rel[_t5_bucket(d)].transpose(2, 0, 1)
    tb_prev = rel[_t5_bucket(d + tq)].transpose(2, 0, 1)

    def far_idx(b, qb, st):
        far_len = jnp.maximum(qb - 1, 0) * tq
        n_far = (far_len + kb - 1) // kb
        return (b, jnp.clip(st - 1, 0, jnp.maximum(n_far - 1, 0)), 0)

    kv_far = pl.BlockSpec((1, kb, N_KV_HEADS * HEAD_DIM), far_idx)
    kv_prev = pl.BlockSpec((1, tq, N_KV_HEADS * HEAD_DIM), lambda b, qb, st: (b, jnp.maximum(qb - 1, 0), 0))
    kv_diag = pl.BlockSpec((1, tq, N_KV_HEADS * HEAD_DIM), lambda b, qb, st: (b, qb, 0))
    tile = pl.BlockSpec((N_HEADS, tq, tq), lambda b, qb, st: (0, 0, 0))
    return pl.pallas_call(
        functools.partial(_dsa_prompt_body, tq=tq, kb=kb),
        grid=(B, nq, n_far_max + 1),
        in_specs=[pl.BlockSpec((1, N_IDX_HEADS, tq, IDX_DIM), lambda b, qb, st: (b, 0, qb, 0)),
                  pl.BlockSpec((1, tq, N_IDX_HEADS), lambda b, qb, st: (b, qb, 0)),
                  pl.BlockSpec((1, IDX_DIM, T), lambda b, qb, st: (b, 0, 0)),
                  pl.BlockSpec((1, tq, N_HEADS * HEAD_DIM), lambda b, qb, st: (b, qb, 0)),
                  kv_far, kv_far, kv_prev, kv_prev, kv_diag, kv_diag, tile, tile],
        out_specs=pl.BlockSpec((1, tq, N_HEADS * HEAD_DIM), lambda b, qb, st: (b, qb, 0)),
        out_shape=jax.ShapeDtypeStruct((B, T, N_HEADS * HEAD_DIM), BF16),
        scratch_shapes=[pltpu.VMEM((tq, T), I32), pltpu.VMEM((tq, 1), I32),
                        pltpu.VMEM((N_HEADS, tq, 1), F32), pltpu.VMEM((N_HEADS, tq, 1), F32),
                        pltpu.VMEM((tq, N_HEADS * HEAD_DIM), F32)],
        compiler_params=_cparams(("parallel", "arbitrary", "arbitrary")),
        name="dsa_prompt",
    )(qi_h, wi, kit, q, k, v, k, v, k, v, tb_diag, tb_prev)


def _mem_attn_body(xq_ref, mk_ref, mv_ref, out_ref):
    scale = X_HEAD_DIM ** -0.5
    for h in range(X_HEADS):
        hs = slice(h * X_HEAD_PAD, (h + 1) * X_HEAD_PAD)
        s = lax.dot_general(xq_ref[0, :, hs], mk_ref[0, :, hs], (((1,), (1,)), ((), ())),
                            preferred_element_type=F32) * scale
        m = jnp.max(s, axis=-1, keepdims=True)
        p = jnp.exp(s - m)
        p = p / jnp.sum(p, axis=-1, keepdims=True)
        out_ref[0, :, hs] = jnp.dot(p.astype(BF16), mv_ref[0, :, hs],
                                    preferred_element_type=F32).astype(out_ref.dtype)


def _mem_attn(xq, mk, mv):
    B, T, W = xq.shape
    M = mk.shape[1]
    tm = min(512, T)
    return pl.pallas_call(
        _mem_attn_body,
        grid=(B, T // tm),
        in_specs=[pl.BlockSpec((1, tm, W), lambda b, i: (b, i, 0)),
                  pl.BlockSpec((1, M, W), lambda b, i: (b, 0, 0)),
                  pl.BlockSpec((1, M, W), lambda b, i: (b, 0, 0))],
        out_specs=pl.BlockSpec((1, tm, W), lambda b, i: (b, i, 0)),
        out_shape=jax.ShapeDtypeStruct((B, T, W), BF16),
        compiler_params=_cparams(("parallel", "parallel")),
        name="mem_attn",
    )(xq, mk, mv)


def _merge_body(x_ref, rnn_ref, att_ref, mem_ref, g_ref, wr_ref, wa_ref, wm_ref, wo_ref, gn_ref,
                x1_ref, hf_ref):
    mixed = _sigmoid(g_ref[:, 0:D_MODEL]) * jnp.dot(rnn_ref[...], wr_ref[...], preferred_element_type=F32)
    mixed = mixed + _sigmoid(g_ref[:, D_MODEL:2 * D_MODEL]) * jnp.dot(att_ref[...], wa_ref[...],
                                                                     preferred_element_type=F32)
    mixed = mixed + _sigmoid(g_ref[:, 2 * D_MODEL:3 * D_MODEL]) * jnp.dot(mem_ref[...], wm_ref[...],
                                                                         preferred_element_type=F32)
    x1 = x_ref[...] + jnp.dot(mixed.astype(BF16), wo_ref[...], preferred_element_type=F32)
    x1_ref[...] = x1
    ms = jnp.mean(x1 * x1, axis=-1, keepdims=True)
    hf_ref[...] = (x1 * lax.rsqrt(ms + EPS) * gn_ref[...]).astype(BF16)


def _merge(x, rnn, att, mem, g, wr, wa, wm, wo, gain):
    M = x.shape[0]
    tm = min(256, M)
    row = lambda w: pl.BlockSpec((tm, w), lambda i: (i, 0))
    full = lambda a: pl.BlockSpec(a.shape, lambda i: (0,) * a.ndim)
    gain = gain.reshape(1, D_MODEL)
    return pl.pallas_call(
        _merge_body,
        grid=(M // tm,),
        in_specs=[row(D_MODEL), row(rnn.shape[1]), row(att.shape[1]), row(mem.shape[1]), row(3 * D_MODEL),
                  full(wr), full(wa), full(wm), full(wo), full(gain)],
        out_specs=[row(D_MODEL), row(D_MODEL)],
        out_shape=[jax.ShapeDtypeStruct((M, D_MODEL), F32), jax.ShapeDtypeStruct((M, D_MODEL), BF16)],
        compiler_params=_cparams(("parallel",)),
        name="merge",
    )(x, rnn, att, mem, g, wr, wa, wm, wo, gain)


def _router_body(h_ref, w_ref, b_ref, pos_ref, gate_ref, cnt_ref, *, tm):
    logits = jnp.dot(h_ref[...], w_ref[...], preferred_element_type=F32) + b_ref[...]
    lane = lax.broadcasted_iota(I32, logits.shape, 1)
    vals, idxs = [], []
    for _ in range(TOP_K):
        m = jnp.max(logits, axis=-1, keepdims=True)
        i = jnp.min(jnp.where(logits == m, lane, jnp.int32(128)), axis=-1, keepdims=True)
        vals.append(m)
        idxs.append(i)
        logits = jnp.where(lane == i, -jnp.inf, logits)
    es = [jnp.exp(v - vals[0]) for v in vals]
    tot = es[0] + es[1] + es[2] + es[3]

    hots = [lane == i for i in idxs]
    hotf = [h.astype(F32) for h in hots]
    cnts = [jnp.sum(h, axis=0, keepdims=True) for h in hotf]
    c_i = (cnts[0] + cnts[1] + cnts[2] + cnts[3]).astype(I32)
    cpad = (((c_i + (SEG_ALIGN - 1)) // SEG_ALIGN) * SEG_ALIGN).astype(F32)
    r128 = lax.broadcasted_iota(I32, (128, 128), 0)
    c128 = lax.broadcasted_iota(I32, (128, 128), 1)
    loff = jnp.dot(jnp.broadcast_to(cpad, (8, 128)).astype(BF16), (r128 < c128).astype(BF16),
                   preferred_element_type=F32)[0:1]
    low = (lax.broadcasted_iota(I32, (tm, tm), 0) > lax.broadcasted_iota(I32, (tm, tm), 1)).astype(BF16)
    before = loff
    po = jnp.zeros(lane.shape, I32)
    go = jnp.zeros(lane.shape, F32)
    for k in range(TOP_K):
        pref = jnp.dot(low, hotf[k].astype(BF16), preferred_element_type=F32)
        pos = jnp.sum(jnp.where(hots[k], before + pref, 0.0), axis=-1, keepdims=True)
        before = before + cnts[k]
        po = jnp.where(lane == k, pos.astype(I32), po)
        go = jnp.where(lane == k, es[k] / tot, go)
    pos_ref[...] = po
    gate_ref[...] = go
    cnt_ref[0] = jnp.broadcast_to(c_i, (8, 128))


def _router(hf, rw, rb, tm):
    M = hf.shape[0]
    nt = M // tm
    return pl.pallas_call(
        functools.partial(_router_body, tm=tm),
        grid=(nt,),
        in_specs=[pl.BlockSpec((tm, D_MODEL), lambda i: (i, 0)),
                  pl.BlockSpec((D_MODEL, 128), lambda i: (0, 0)),
                  pl.BlockSpec((1, 128), lambda i: (0, 0))],
        out_specs=[pl.BlockSpec((tm, 128), lambda i: (i, 0)), pl.BlockSpec((tm, 128), lambda i: (i, 0)),
                   pl.BlockSpec((1, 8, 128), lambda i: (i, 0, 0))],
        out_shape=[jax.ShapeDtypeStruct((M, 128), I32), jax.ShapeDtypeStruct((M, 128), F32),
                   jax.ShapeDtypeStruct((nt, 8, 128), I32)],
        compiler_params=_cparams(("parallel",)),
        name="router",
    )(hf, rw, rb)


def _segment_copies(tile, n8_ref, loff_ref, gdst_ref, vm_ref, hbm_ref, sem, nbits, to_hbm, start):
    def body(e, carry):
        j = tile * N_EXPERTS + e
        n8, lo, gd = n8_ref[j], loff_ref[j], gdst_ref[j]
        for b in range(nbits):
            size = SEG_ALIGN << b
            off = (n8 & ((1 << b) - 1)) * SEG_ALIGN

            @pl.when(((n8 >> b) & 1) == 1)
            def _():
                v = vm_ref.at[pl.ds(pl.multiple_of(lo + off, SEG_ALIGN), size)]
                h = hbm_ref.at[pl.ds(pl.multiple_of(gd + off, SEG_ALIGN), size)]
                cp = pltpu.make_async_copy(v, h, sem) if to_hbm else pltpu.make_async_copy(h, v, sem)
                if start:
                    cp.start()
                else:
                    cp.wait()
        return carry

    lax.fori_loop(0, N_EXPERTS, body, 0)


def _dispatch_body(n8_ref, loff_ref, gdst_ref, pos_ref, h_ref, xinit_ref, xbuf_ref, xs_ref, sem, *, tm, rows, nbits):
    del xinit_ref
    tile = pl.program_id(0)
    pos_t = pos_ref[...].astype(F32).T
    sub = lax.broadcasted_iota(I32, (rows, tm), 0).astype(F32)
    perm = jnp.zeros((rows, tm), F32)
    for k in range(TOP_K):
        perm = perm + jnp.where(sub == pos_t[k:k + 1, :], 1.0, 0.0)
    xs_ref[...] = jnp.dot(perm.astype(BF16), h_ref[...], preferred_element_type=F32)
    _segment_copies(tile, n8_ref, loff_ref, gdst_ref, xs_ref, xbuf_ref, sem, nbits, True, True)
    _segment_copies(tile, n8_ref, loff_ref, gdst_ref, xs_ref, xbuf_ref, sem, nbits, True, False)


def _combine_body(n8_ref, loff_ref, gdst_ref, pos_ref, gate_ref, x1_ref, ybuf_ref, out_ref, ys_ref, sem,
                  *, tm, rows, nbits):
    tile = pl.program_id(0)

    @pl.when(tile == 0)
    def _():
        ys_ref[...] = jnp.zeros_like(ys_ref)

    _segment_copies(tile, n8_ref, loff_ref, gdst_ref, ys_ref, ybuf_ref, sem, nbits, False, True)
    lane = lax.broadcasted_iota(I32, (tm, rows), 1)
    pos = pos_ref[...]
    gate = gate_ref[...]
    g = jnp.zeros((tm, rows), F32)
    for k in range(TOP_K):
        g = g + jnp.where(lane == pos[:, k:k + 1], gate[:, k:k + 1], 0.0)
    g_hi = g.astype(BF16)
    g_lo = (g - g_hi.astype(F32)).astype(BF16)
    _segment_copies(tile, n8_ref, loff_ref, gdst_ref, ys_ref, ybuf_ref, sem, nbits, False, False)
    ys = ys_ref[...]
    y_hi = ys.astype(BF16)
    y_lo = (ys - y_hi.astype(F32)).astype(BF16)
    acc = jnp.dot(g_hi, y_hi, preferred_element_type=F32)
    acc = acc + jnp.dot(g_hi, y_lo, preferred_element_type=F32)
    acc = acc + jnp.dot(g_lo, y_hi, preferred_element_type=F32)
    out_ref[...] = x1_ref[...] + acc


def _expert_body(be_ref, nu_ref, x_ref, wgu_ref, bgu_ref, wdn_ref, bdn_ref, y_ref):
    @pl.when(pl.program_id(0) < nu_ref[0])
    def _():
        hgu = jnp.dot(x_ref[...].astype(BF16), wgu_ref[0], preferred_element_type=F32) + bgu_ref[0]
        gate = jnp.minimum(hgu[:, :D_FF], SWIGLU_LIMIT)
        up = jnp.clip(hgu[:, D_FF:], -SWIGLU_LIMIT, SWIGLU_LIMIT)
        act = (up + 1.0) * gate * _sigmoid(SWIGLU_ALPHA * gate)
        y_ref[...] = jnp.dot(act.astype(BF16), wdn_ref[0], preferred_element_type=F32) + bdn_ref[0]

    @pl.when(pl.program_id(0) >= nu_ref[0])
    def _():
        y_ref[...] = jnp.zeros_like(y_ref)


def _experts(xbuf, blk_e, n_used, wgu, bgu, wdn, bdn):
    nb = blk_e.shape[0]
    rb = xbuf.shape[0] // nb
    last = lambda i, nu: jnp.minimum(i, jnp.maximum(nu[0] - 1, 0))
    grid_spec = pltpu.PrefetchScalarGridSpec(
        num_scalar_prefetch=2,
        grid=(nb,),
        in_specs=[pl.BlockSpec((rb, D_MODEL), lambda i, be, nu: (last(i, nu), 0)),
                  pl.BlockSpec((1, D_MODEL, 2 * D_FF), lambda i, be, nu: (be[i], 0, 0)),
                  pl.BlockSpec((1, 1, 2 * D_FF), lambda i, be, nu: (be[i], 0, 0)),
                  pl.BlockSpec((1, D_FF, D_MODEL), lambda i, be, nu: (be[i], 0, 0)),
                  pl.BlockSpec((1, 1, D_MODEL), lambda i, be, nu: (be[i], 0, 0))],
        out_specs=pl.BlockSpec((rb, D_MODEL), lambda i, be, nu: (i, 0)),
    )
    return pl.pallas_call(
        _expert_body,
        grid_spec=grid_spec,
        out_shape=jax.ShapeDtypeStruct((nb * rb, D_MODEL), F32),
        compiler_params=_cparams(("arbitrary",)),
        name="experts",
    )(blk_e, n_used, xbuf, wgu, bgu.reshape(N_EXPERTS, 1, 2 * D_FF), wdn, bdn.reshape(N_EXPERTS, 1, D_MODEL))


def _moe(hf, x1, W):
    M = hf.shape[0]
    tm = min(MOE_TM, M)
    nt = M // tm
    assert M % tm == 0 and tm * TOP_K // SEG_ALIGN <= 256
    pos, gates, cnt = _router(hf, W['router_w'], W['router_b'], tm)

    rb = MOE_RB if M * TOP_K >= N_EXPERTS * MOE_RB else 128
    cpad = (cnt[:, 0, :N_EXPERTS] + (SEG_ALIGN - 1)) // SEG_ALIGN * SEG_ALIGN
    loff = jnp.cumsum(cpad, axis=1) - cpad
    region = (jnp.sum(cpad, axis=0) + rb - 1) // rb * rb
    pend = jnp.cumsum(region)
    gdst = (pend - region)[None, :] + jnp.cumsum(cpad, axis=0) - cpad
    max_rows = M * TOP_K + nt * N_EXPERTS * (SEG_ALIGN - 1)
    nb = (max_rows + rb - 1) // rb + N_EXPERTS
    blk_e = jnp.minimum(jnp.searchsorted(pend, jnp.arange(nb, dtype=I32) * rb, side='right'),
                        N_EXPERTS - 1).astype(I32)
    n_used = (pend[-1] // rb).astype(I32).reshape(1)
    plan = ((cpad // SEG_ALIGN).reshape(-1).astype(I32), loff.reshape(-1).astype(I32),
            gdst.reshape(-1).astype(I32))

    rows = (tm * TOP_K + N_EXPERTS * (SEG_ALIGN - 1) + 127) // 128 * 128
    nbits = (tm * TOP_K // SEG_ALIGN).bit_length()
    any_spec = pl.BlockSpec(memory_space=pl.ANY)
    row = lambda w: pl.BlockSpec((tm, w), lambda i, *_: (i, 0))
    xbuf = pl.pallas_call(
        functools.partial(_dispatch_body, tm=tm, rows=rows, nbits=nbits),
        grid_spec=pltpu.PrefetchScalarGridSpec(
            num_scalar_prefetch=3, grid=(nt,),
            in_specs=[row(128), row(D_MODEL), any_spec],
            out_specs=any_spec,
            scratch_shapes=[pltpu.VMEM((rows, D_MODEL), F32), pltpu.SemaphoreType.DMA(())]),
        out_shape=jax.ShapeDtypeStruct((nb * rb, D_MODEL), F32),
        input_output_aliases={5: 0},
        compiler_params=_cparams(("arbitrary",)),
        name="moe_dispatch",
    )(*plan, pos, hf, jnp.zeros((nb * rb, D_MODEL), F32))
    ybuf = _experts(xbuf, blk_e, n_used, W['exp_w_gu'], W['exp_b_gu'], W['exp_w_down'], W['exp_b_down'])
    return pl.pallas_call(
        functools.partial(_combine_body, tm=tm, rows=rows, nbits=nbits),
        grid_spec=pltpu.PrefetchScalarGridSpec(
            num_scalar_prefetch=3, grid=(nt,),
            in_specs=[row(128), row(128), row(D_MODEL), any_spec],
            out_specs=row(D_MODEL),
            scratch_shapes=[pltpu.VMEM((rows, D_MODEL), F32), pltpu.SemaphoreType.DMA(())]),
        out_shape=jax.ShapeDtypeStruct((M, D_MODEL), F32),
        compiler_params=_cparams(("arbitrary",)),
        name="moe_combine",
    )(*plan, pos, gates, x1, ybuf)


_NT = (((1,), (1,)), ((), ()))


def _dsa_step_body(pt_ref, qi_ref, wi_ref, kin_ref, q_ref, kn_ref, vn_ref, tbl_ref, bnew_ref, hsel_ref, exp_ref,
                   *refs, g, n_pages):
    del pt_ref
    ki_refs, k_refs, v_refs = refs[:g], refs[g:2 * g], refs[2 * g:3 * g]
    out_ref, s_ref, thr_ref, knew_ref, m_ref, l_ref, acc_ref = refs[3 * g:]
    j = pl.program_id(1)
    ns = n_pages // g
    qi = qi_ref[0]
    wi = wi_ref[0]

    @pl.when(j < ns)
    def _():
        for t in range(g):
            s = lax.dot_general(qi, ki_refs[t][0].astype(BF16), _NT, preferred_element_type=F32)
            row = jnp.sum(jnp.maximum(s, 0.0) * wi, axis=0, keepdims=True)
            s_ref[pl.ds(j * g + t, 1), :] = _sortable_key(row)

    @pl.when(j == ns - 1)
    def _():
        sn = jnp.sum(qi.astype(F32) * kin_ref[0].astype(BF16).astype(F32), axis=-1, keepdims=True)
        knew = _sortable_key(jnp.sum(jnp.maximum(sn, 0.0) * wi, axis=0, keepdims=True))

        def total(x):
            return jnp.sum(jnp.sum(x, axis=0, keepdims=True), axis=1, keepdims=True)

        def count_ge(cand):
            return total((s_ref[...] >= cand).astype(I32)) + (knew >= cand).astype(I32)

        base, cnt = _topk_threshold(count_ge, 1)
        thr_ref[...] = jnp.maximum(base, jnp.int32(INT_MIN + 1))
        knew_ref[...] = knew
        tie = jnp.logical_and(base > INT_MIN, cnt > TOPK)

        @pl.when(jnp.max(tie.astype(I32)) > 0)
        def _():
            need = (TOPK - count_ge(base + 1)).astype(F32)
            keys = s_ref[...]
            eq = (keys == base).astype(F32)
            incl = (lax.broadcasted_iota(I32, (128, 128), 0) <= lax.broadcasted_iota(I32, (128, 128), 1))
            lane_rank = jnp.dot(eq.astype(BF16), incl.astype(BF16), preferred_element_type=F32)
            low = (lax.broadcasted_iota(I32, (n_pages, n_pages), 0) > lax.broadcasted_iota(I32, (n_pages, n_pages), 1))
            row_tot = jnp.broadcast_to(jnp.sum(eq, axis=1, keepdims=True), eq.shape)
            row_off = jnp.dot(low.astype(BF16), row_tot.astype(BF16), preferred_element_type=F32)
            s_ref[...] = jnp.where(eq * (lane_rank + row_off - need) > 0.0, base - 1, keys)
            new_demoted = jnp.logical_and(knew == base, total(eq) + 1.0 > need)
            knew_ref[...] = jnp.where(new_demoted, base - 1, knew)

        m_ref[...] = jnp.full(m_ref.shape, NEG, F32)
        l_ref[...] = jnp.zeros_like(l_ref)
        acc_ref[...] = jnp.zeros_like(acc_ref)

    @pl.when(j >= ns)
    def _():
        q8 = q_ref[0]
        p0 = pl.multiple_of((j - ns) * g, g)
        sel = (s_ref[pl.ds(p0, g), :] >= thr_ref[...]).astype(F32).astype(BF16)
        sel = jnp.dot(sel, exp_ref[...], preferred_element_type=F32)
        blocks = []
        for t in range(g):
            s = lax.dot_general(q8, k_refs[t][...].astype(BF16), _NT, preferred_element_type=F32)
            if t == g - 1:
                s = s + jnp.where(j == 2 * ns - 1, tbl_ref[...], 0.0)
            blocks.append(jnp.where(sel[t:t + 1, :] * hsel_ref[...] > 0.5, s, NEG))
        s_all = jnp.concatenate(blocks, axis=-1)
        m_old = m_ref[...]
        m_new = jnp.maximum(m_old, jnp.max(s_all, axis=-1, keepdims=True))
        alpha = jnp.exp2(m_old - m_new)
        pr = jnp.exp2(s_all - m_new)
        l_ref[...] = alpha * l_ref[...] + jnp.sum(pr, axis=-1, keepdims=True)
        acc = alpha * acc_ref[...]
        w = PAGE_SIZE * N_KV_HEADS
        for t in range(g):
            acc = acc + jnp.dot(pr[:, t * w:(t + 1) * w].astype(BF16), v_refs[t][...].astype(BF16),
                                preferred_element_type=F32)
        acc_ref[...] = acc
        m_ref[...] = m_new

    @pl.when(j == 2 * ns - 1)
    def _():
        kn = kn_ref[0].astype(BF16).astype(F32)
        s_new = jnp.sum(q_ref[0].astype(F32) * kn, axis=-1, keepdims=True) + bnew_ref[...]
        s_new = jnp.where(knew_ref[...] >= thr_ref[...], s_new, NEG)
        m_old = m_ref[...]
        m_new = jnp.maximum(m_old, s_new)
        alpha = jnp.exp2(m_old - m_new)
        pn = jnp.exp2(s_new - m_new)
        l = alpha * l_ref[...] + pn
        o = (alpha * acc_ref[...] + pn * vn_ref[0].astype(BF16).astype(F32)) / l
        out_ref[0] = o.astype(out_ref.dtype)


def _dsa_step(q, k_new, v_new, qi, wi, ki_new, cache_k, cache_v, cache_kidx, page_table, rel_bias):
    Bd, n_pages = page_table.shape
    past = n_pages * PAGE_SIZE
    g = min(DEC_G, n_pages)
    assert n_pages % g == 0 and PAGE_SIZE >= _FAR_DIST and TOPK <= (past + 1) // 4
    ns = n_pages // g
    n_phys = cache_k.shape[0]
    kvw = N_KV_HEADS * HEAD_DIM
    rel = (rel_bias - rel_bias[N_BUCKETS - 1]).astype(F32) * LOG2E
    pw = PAGE_SIZE * N_KV_HEADS
    col_tok = jnp.arange(pw, dtype=I32) // N_KV_HEADS
    col_kvh = jnp.arange(pw, dtype=I32) % N_KV_HEADS
    kvh_of_head = jnp.arange(N_HEADS, dtype=I32) // (N_HEADS // N_KV_HEADS)
    tbl = rel[_t5_bucket(PAGE_SIZE - col_tok)].T
    bnew = rel[_t5_bucket(jnp.zeros((1,), I32))].T
    hsel = (kvh_of_head[:, None] == col_kvh[None, :]).astype(F32)
    expand = (jnp.arange(PAGE_SIZE, dtype=I32)[:, None] == col_tok[None, :]).astype(BF16)
    kn8 = k_new.reshape(Bd, N_KV_HEADS, HEAD_DIM)[:, kvh_of_head]
    vn8 = v_new.reshape(Bd, N_KV_HEADS, HEAD_DIM)[:, kvh_of_head]

    per_seq = lambda shape: pl.BlockSpec((1,) + shape, lambda b, j, pt: (b, 0, 0))
    full = lambda shape: pl.BlockSpec(shape, lambda b, j, pt: (0, 0))

    def ki_page(t):
        return pl.BlockSpec((1, PAGE_SIZE, IDX_DIM),
                            lambda b, j, pt: (pt[b * n_pages + jnp.where(j < ns, j * g + t, (ns - 1) * g + t)], 0, 0))

    def kv_page(t):
        return pl.BlockSpec((pw, HEAD_DIM),
                            lambda b, j, pt: (pt[b * n_pages + jnp.where(j < ns, t, (j - ns) * g + t)], 0))

    grid_spec = pltpu.PrefetchScalarGridSpec(
        num_scalar_prefetch=1, grid=(Bd, 2 * ns),
        in_specs=[per_seq((N_IDX_HEADS, IDX_DIM)), per_seq((N_IDX_HEADS, 1)), per_seq((1, IDX_DIM)),
                  per_seq((N_HEADS, HEAD_DIM)), per_seq((N_HEADS, HEAD_DIM)), per_seq((N_HEADS, HEAD_DIM)),
                  full((N_HEADS, pw)), full((N_HEADS, 1)), full((N_HEADS, pw)), full((PAGE_SIZE, pw))]
                 + [ki_page(t) for t in range(g)] + [kv_page(t) for t in range(g)] + [kv_page(t) for t in range(g)],
        out_specs=per_seq((N_HEADS, HEAD_DIM)),
        scratch_shapes=[pltpu.VMEM((n_pages, PAGE_SIZE), I32), pltpu.VMEM((1, 1), I32), pltpu.VMEM((1, 1), I32),
                        pltpu.VMEM((N_HEADS, 1), F32), pltpu.VMEM((N_HEADS, 1), F32),
                        pltpu.VMEM((N_HEADS, HEAD_DIM), F32)])
    ck = cache_k.reshape(n_phys * pw, HEAD_DIM)
    cv = cache_v.reshape(n_phys * pw, HEAD_DIM)
    out = pl.pallas_call(
        functools.partial(_dsa_step_body, g=g, n_pages=n_pages),
        grid_spec=grid_spec,
        out_shape=jax.ShapeDtypeStruct((Bd, N_HEADS, HEAD_DIM), BF16),
        compiler_params=_cparams(("arbitrary", "arbitrary")),
        name="dsa_step",
    )(page_table.reshape(-1), qi.reshape(Bd, N_IDX_HEADS, IDX_DIM), wi.reshape(Bd, N_IDX_HEADS, 1),
      ki_new.reshape(Bd, 1, IDX_DIM), q.reshape(Bd, N_HEADS, HEAD_DIM), kn8, vn8, tbl, bnew, hsel, expand,
      *([cache_kidx] * g), *([ck] * g), *([cv] * g))
    return out.reshape(Bd, N_HEADS * HEAD_DIM)


def _mem_step_body(xq_ref, mk_ref, mv_ref, sel_ref, out_ref):
    s = lax.dot_general(xq_ref[0], mk_ref[0].astype(BF16), _NT, preferred_element_type=F32) * X_HEAD_DIM ** -0.5
    p = jnp.exp(s - jnp.max(s, axis=-1, keepdims=True))
    p = p / jnp.sum(p, axis=-1, keepdims=True)
    r = jnp.dot(p.astype(BF16), mv_ref[0].astype(BF16), preferred_element_type=F32)
    out_ref[0] = jnp.sum(r * sel_ref[...], axis=0, keepdims=True).astype(out_ref.dtype)


def _mem_step(xq, mem_k, mem_v):
    Bd, M = mem_k.shape[:2]
    w = X_HEADS * X_HEAD_DIM
    eye = jnp.eye(X_HEADS, dtype=xq.dtype)
    xbd = (xq[:, :, None, :] * eye[None, :, :, None]).reshape(Bd, X_HEADS, w)
    xbd = jnp.pad(xbd, ((0, 0), (0, 8 - X_HEADS), (0, 0)))
    sel = jnp.pad(jnp.repeat(jnp.eye(X_HEADS, dtype=F32), X_HEAD_DIM, axis=1), ((0, 8 - X_HEADS), (0, 0)))
    out = pl.pallas_call(
        _mem_step_body,
        grid=(Bd,),
        in_specs=[pl.BlockSpec((1, 8, w), lambda b: (b, 0, 0)), pl.BlockSpec((1, M, w), lambda b: (b, 0, 0)),
                  pl.BlockSpec((1, M, w), lambda b: (b, 0, 0)), pl.BlockSpec((8, w), lambda b: (0, 0))],
        out_specs=pl.BlockSpec((1, 1, w), lambda b: (b, 0, 0)),
        out_shape=jax.ShapeDtypeStruct((Bd, 1, w), BF16),
        compiler_params=_cparams(("parallel",)),
        name="mem_step",
    )(xbd, mem_k.reshape(Bd, M, w), mem_v.reshape(Bd, M, w), sel)
    return out.reshape(Bd, w)


def _pad_heads_cols(w):
    K = w.shape[0]
    w = w.reshape(K, X_HEADS, X_HEAD_DIM)
    return jnp.pad(w, ((0, 0), (0, 0), (0, X_HEAD_PAD - X_HEAD_DIM))).reshape(K, X_HEADS * X_HEAD_PAD)


def _pad_heads_vec(g):
    return jnp.pad(g, (0, X_HEAD_PAD - X_HEAD_DIM))


def _prep_weights(P):
    W = {}
    w_in = P['w_in'].astype(BF16)
    widths = (D_RNN, D_RNN, N_HEADS * HEAD_DIM, N_KV_HEADS * HEAD_DIM, N_KV_HEADS * HEAD_DIM,
              N_IDX_HEADS * IDX_DIM, IDX_DIM, N_IDX_HEADS, X_HEADS * X_HEAD_DIM, 3 * D_MODEL)
    offs = np.cumsum((0,) + widths)
    seg = lambda i: w_in[:, offs[i]:offs[i + 1]]
    W['w_xy'] = w_in[:, offs[0]:offs[2]]
    W['w_q'], W['w_k'], W['w_v'], W['w_qi'] = seg(2), seg(3), seg(4), seg(5)
    W['w_kiwi'] = jnp.pad(w_in[:, offs[6]:offs[8]], ((0, 0), (0, 128 - IDX_DIM - N_IDX_HEADS)))
    W['w_xq'] = _pad_heads_cols(seg(8))
    W['w_g'] = seg(9)
    W['q_gain'] = jnp.tile(P['q_norm'], N_HEADS)
    W['q_gain_log2'] = W['q_gain'] * (HEAD_DIM ** -0.5 * LOG2E)
    W['k_gain'] = jnp.tile(P['k_norm'], N_KV_HEADS)
    W['xq_gain'] = jnp.tile(_pad_heads_vec(P['xq_norm']), X_HEADS)
    W['xk_gain'] = jnp.tile(_pad_heads_vec(P['xk_norm']), X_HEADS)
    wmk, wmv = jnp.split(P['w_mem_kv'].astype(BF16), 2, axis=-1)
    W['w_mem_k'] = _pad_heads_cols(wmk)
    W['w_mem_v'] = _pad_heads_cols(wmv)
    W['wg_lru'] = _block_diag_gates(P['lru_wa'], P['lru_wx'])
    W['w_br_rnn'] = P['w_br_rnn'].astype(BF16)
    W['w_br_attn'] = P['w_br_attn'].astype(BF16)
    W['w_br_mem'] = _pad_heads_cols(P['w_br_mem'].astype(BF16).T).T
    W['w_out'] = P['w_out'].astype(BF16)
    W['router_w'] = jnp.pad(P['router_w'].astype(BF16), ((0, 0), (0, 128 - N_EXPERTS)))
    W['router_b'] = jnp.pad(P['router_b'].astype(F32), (0, 128 - N_EXPERTS), constant_values=-1e30).reshape(1, 128)
    W['exp_w_gu'] = P['exp_w_gu'].astype(BF16)
    W['exp_w_down'] = P['exp_w_down'].astype(BF16)
    for name in ('norm_mix', 'conv_w', 'conv_b', 'lru_ba', 'lru_bx', 'lru_lambda', 'rel_bias', 'mem_norm',
                 'norm_ffn', 'exp_b_gu', 'exp_b_down'):
        W[name] = P[name]
    return W


def _unpad_heads(a):
    return a.reshape(a.shape[:-1] + (X_HEADS, X_HEAD_PAD))[..., :X_HEAD_DIM]


def _in_proj(x2, W, q_gain):
    g = W['norm_mix']
    o = {}
    (o['xy'],) = _norm_proj(x2, g, W['w_xy'], [F32], tn=1024)
    (o['q'],) = _norm_proj(x2, g, W['w_q'], [BF16], tn=512, mode="headnorm", head_gain=q_gain)
    o['k'], o['k16'] = _norm_proj(x2, g, W['w_k'], [F32, BF16], tn=512, mode="headnorm", head_gain=W['k_gain'])
    o['v'], o['v16'] = _norm_proj(x2, g, W['w_v'], [F32, BF16], tn=512)
    (o['qi'],) = _norm_proj(x2, g, W['w_qi'], [BF16], tn=512)
    (o['kiwi'],) = _norm_proj(x2, g, W['w_kiwi'], [F32], tn=128)
    (o['xq'],) = _norm_proj(x2, g, W['w_xq'], [BF16], tn=512, mode="headnorm", head_gain=W['xq_gain'],
                            hd_pad=X_HEAD_PAD, hd_true=X_HEAD_DIM)
    (o['g'],) = _norm_proj(x2, g, W['w_g'], [F32], tn=1024)
    return o


def kernel(x_prompt, x_sample, cache_k, cache_v, cache_kidx, cache_mem_k, cache_mem_v, state_conv, state_rglru, page_table, mem_prompt, norm_mix, w_in, conv_w, conv_b, lru_wa, lru_ba, lru_wx, lru_bx, lru_lambda, q_norm, k_norm, rel_bias, mem_norm, w_mem_kv, xq_norm, xk_norm, w_br_rnn, w_br_attn, w_br_mem, w_out, norm_ffn, router_w, router_b, exp_w_gu, exp_b_gu, exp_w_down, exp_b_down):
    W = _prep_weights(dict(norm_mix=norm_mix, w_in=w_in, conv_w=conv_w, conv_b=conv_b, lru_wa=lru_wa,
                           lru_ba=lru_ba, lru_wx=lru_wx, lru_bx=lru_bx, lru_lambda=lru_lambda, q_norm=q_norm,
                           k_norm=k_norm, rel_bias=rel_bias, mem_norm=mem_norm, w_mem_kv=w_mem_kv,
                           xq_norm=xq_norm, xk_norm=xk_norm, w_br_rnn=w_br_rnn, w_br_attn=w_br_attn,
                           w_br_mem=w_br_mem, w_out=w_out, norm_ffn=norm_ffn, router_w=router_w,
                           router_b=router_b, exp_w_gu=exp_w_gu, exp_b_gu=exp_b_gu, exp_w_down=exp_w_down,
                           exp_b_down=exp_b_down))
    lru = (W['conv_w'], W['conv_b'], W['wg_lru'], W['lru_ba'], W['lru_bx'], W['lru_lambda'])

    B, T, D = x_prompt.shape
    xp2 = x_prompt.reshape(B * T, D)
    memp2 = mem_prompt.reshape(B * N_MEM, D)
    mk_pad, mk16 = _norm_proj(memp2, W['mem_norm'], W['w_mem_k'], [F32, BF16], tn=512, mode="headnorm",
                              head_gain=W['xk_gain'], hd_pad=X_HEAD_PAD, hd_true=X_HEAD_DIM)
    mv_pad, mv16 = _norm_proj(memp2, W['mem_norm'], W['w_mem_v'], [F32, BF16], tn=512)
    mk_p = _unpad_heads(mk_pad).reshape(B, N_MEM, X_HEADS, X_HEAD_DIM)
    mv_p = _unpad_heads(mv_pad).reshape(B, N_MEM, X_HEADS, X_HEAD_DIM)

    o = _in_proj(xp2, W, W['q_gain_log2'])
    xy = o['xy'].reshape(B, T, 2 * D_RNN)
    rnn_out, rg_p = _rglru_prompt(xy, *lru)
    conv_p = xy[:, T - (CONV_W - 1):, :D_RNN]
    ki_p = o['kiwi'][:, :IDX_DIM].reshape(B, T, IDX_DIM)
    wi_p = o['kiwi'][:, IDX_DIM:IDX_DIM + N_IDX_HEADS].reshape(B, T, N_IDX_HEADS)
    qi_h = o['qi'].reshape(B, T, N_IDX_HEADS, IDX_DIM).transpose(0, 2, 1, 3)
    kit = ki_p.astype(BF16).transpose(0, 2, 1)
    attn_out = _dsa_prompt(o['q'].reshape(B, T, -1), o['k16'].reshape(B, T, -1), o['v16'].reshape(B, T, -1),
                           qi_h, wi_p, kit, W['rel_bias'])
    mem_out = _mem_attn(o['xq'].reshape(B, T, -1), mk16.reshape(B, N_MEM, -1), mv16.reshape(B, N_MEM, -1))
    x1, hf = _merge(xp2, rnn_out.reshape(B * T, -1), attn_out.reshape(B * T, -1), mem_out.reshape(B * T, -1),
                    o['g'], W['w_br_rnn'], W['w_br_attn'], W['w_br_mem'], W['w_out'], W['norm_ffn'])
    y_prompt = _moe(hf, x1, W).reshape(B, T, D)
    k_p = o['k'].reshape(B, T, N_KV_HEADS, HEAD_DIM)
    v_p = o['v'].reshape(B, T, N_KV_HEADS, HEAD_DIM)

    Bd, S, _ = x_sample.shape
    xs2 = x_sample.reshape(Bd * S, D)
    os_ = _in_proj(xs2, W, W['q_gain_log2'])
    rnn_s, rg_s = _rglru_step(os_['xy'], state_conv.transpose(1, 0, 2), state_rglru, *lru)
    conv_s = jnp.concatenate([state_conv[:, 1:], os_['xy'][:, None, :D_RNN]], axis=1)
    k_s = os_['k'].reshape(Bd, S, N_KV_HEADS, HEAD_DIM)
    v_s = os_['v'].reshape(Bd, S, N_KV_HEADS, HEAD_DIM)
    ki_s = os_['kiwi'][:, :IDX_DIM].reshape(Bd, S, IDX_DIM)
    wi_s = os_['kiwi'][:, IDX_DIM:IDX_DIM + N_IDX_HEADS].reshape(Bd, S, N_IDX_HEADS)
    assert S == 1
    attn_s = _dsa_step(os_['q'], os_['k'], os_['v'], os_['qi'], wi_s.reshape(Bd, -1),
                       ki_s.reshape(Bd, -1), cache_k, cache_v, cache_kidx, page_table, W['rel_bias'])
    mem_s = _mem_step(_unpad_heads(os_['xq']), cache_mem_k, cache_mem_v)
    mem_s = jnp.pad(mem_s.reshape(Bd, X_HEADS, X_HEAD_DIM),
                    ((0, 0), (0, 0), (0, X_HEAD_PAD - X_HEAD_DIM))).reshape(Bd, -1)
    x1s, hfs = _merge(xs2, rnn_s, attn_s, mem_s, os_['g'],
                      W['w_br_rnn'], W['w_br_attn'], W['w_br_mem'], W['w_out'], W['norm_ffn'])
    y_sample = _moe(hfs, x1s, W).reshape(Bd, S, D)

    return (y_prompt, y_sample, k_p, v_p, ki_p, mk_p, mv_p, conv_p, rg_p.reshape(B, D_RNN),
            k_s, v_s, ki_s, conv_s, rg_s)
```

```python
import functools
import math

import numpy as np
import jax
import jax.numpy as jnp
from jax import lax
from jax.experimental import pallas as pl
from jax.experimental.pallas import tpu as pltpu

F32 = jnp.float32
BF16 = jnp.bfloat16
I32 = jnp.int32

D_MODEL = 1024
D_RNN = 1024
LRU_BLOCKS = 16
LRU_BW = 64
CONV_W = 4
LRU_C = 8.0
N_HEADS = 8
N_KV_HEADS = 4
HEAD_DIM = 128
N_IDX_HEADS = 8
IDX_DIM = 64
TOPK = 256
PAGE_SIZE = 128
N_MEM = 256
X_HEADS = 4
X_HEAD_DIM = 192
X_HEAD_PAD = 256
N_BUCKETS = 32
MAX_DISTANCE = 128
N_EXPERTS = 32
TOP_K = 4
D_FF = 1024
SWIGLU_LIMIT = 7.0
SWIGLU_ALPHA = 1.702
EPS = 1e-6

INT_MIN = -2 ** 31
NEG = -1e30
LOG2E = 1.4426950408889634
VMEM_LIMIT = 56 * 1024 * 1024

DSA_TQ = 256
DSA_KB = 2048
DEC_G = 16
MOE_RB = 256
MOE_TM = 256
SEG_ALIGN = 8


def _cparams(sem):
    return pltpu.CompilerParams(dimension_semantics=sem, vmem_limit_bytes=VMEM_LIMIT)


def _sigmoid(x):
    return 1.0 / (1.0 + jnp.exp(-x))


def _gelu_tanh(x):
    return 0.5 * x * (1.0 + jnp.tanh(0.7978845608028654 * (x + 0.044715 * x * x * x)))


def _head_norm(y, gain, hd_pad, hd_true):
    parts = []
    for s in range(0, y.shape[1], hd_pad):
        ys = y[:, s:s + hd_pad]
        ms = jnp.sum(ys * ys, axis=-1, keepdims=True) * (1.0 / hd_true)
        parts.append(ys * lax.rsqrt(ms + EPS) * gain[:, s:s + hd_pad])
    return parts[0] if len(parts) == 1 else jnp.concatenate(parts, axis=-1)


def _proj_body(x_ref, g_ref, w_ref, *refs, mode, hd_pad, hd_true, n_out):
    if mode == "headnorm":
        gain_ref, refs = refs[0], refs[1:]
    outs, h_ref = refs[:n_out], refs[n_out]

    @pl.when(pl.program_id(1) == 0)
    def _():
        x = x_ref[...]
        ms = jnp.mean(x * x, axis=-1, keepdims=True)
        h_ref[...] = (x * lax.rsqrt(ms + EPS) * g_ref[...]).astype(BF16)

    y = jnp.dot(h_ref[...], w_ref[...], preferred_element_type=F32)
    if mode == "headnorm":
        y = _head_norm(y, gain_ref[...], hd_pad, hd_true)
    for o in outs:
        o[...] = y.astype(o.dtype)


def _norm_proj(x, gain, w, out_dtypes, *, tn, mode="plain", head_gain=None, hd_pad=128, hd_true=128):
    M, K = x.shape
    N = w.shape[1]
    tm = min(512, M)
    assert M % tm == 0 and N % tn == 0
    in_specs = [pl.BlockSpec((tm, K), lambda i, j: (i, 0)),
                pl.BlockSpec((1, K), lambda i, j: (0, 0)),
                pl.BlockSpec((K, tn), lambda i, j: (0, j))]
    args = [x, gain.reshape(1, K), w]
    if mode == "headnorm":
        in_specs.append(pl.BlockSpec((1, tn), lambda i, j: (0, j)))
        args.append(head_gain.reshape(1, N))
    body = functools.partial(_proj_body, mode=mode, hd_pad=hd_pad, hd_true=hd_true, n_out=len(out_dtypes))
    return pl.pallas_call(
        body,
        grid=(M // tm, N // tn),
        in_specs=in_specs,
        out_specs=[pl.BlockSpec((tm, tn), lambda i, j: (i, j)) for _ in out_dtypes],
        out_shape=[jax.ShapeDtypeStruct((M, N), dt) for dt in out_dtypes],
        scratch_shapes=[pltpu.VMEM((tm, K), BF16)],
        compiler_params=_cparams(("parallel", "arbitrary")),
        name="norm_proj_" + mode,
    )(*args)


def _softplus(z):
    return jnp.maximum(z, 0.0) + jnp.log1p(jnp.exp(-jnp.abs(z)))


def _lru_terms(xc, wg_ref, ba, bx, lam, a_out, b_out):
    sp = _softplus(-lam)
    for gi in range(4):
        sl = slice(gi * 256, (gi + 1) * 256)
        xg = xc[:, sl]
        z = jnp.dot(xg.astype(BF16), wg_ref[gi], preferred_element_type=F32)
        r = _sigmoid(z[:, :256] + ba[:, sl])
        ig = _sigmoid(z[:, 256:] + bx[:, sl])
        log_a = -LRU_C * r * sp[:, sl]
        a_out[:, sl] = jnp.exp(log_a)
        b_out[:, sl] = jnp.sqrt(1.0 - jnp.exp(2.0 * log_a)) * ig * xg


def _rglru_prompt_body(xr_ref, yr_ref, cw_ref, cb_ref, wg_ref, ba_ref, bx_ref, lam_ref,
                       out_ref, hl_ref, xp_ref, a_ref, b_ref, h_ref, hc_ref, *, tc):
    t = pl.program_id(1)

    @pl.when(t == 0)
    def _():
        xp_ref[0:8, :] = jnp.zeros((8, D_RNN), F32)
        hc_ref[...] = jnp.zeros_like(hc_ref)

    @pl.when(t > 0)
    def _():
        xp_ref[0:8, :] = xp_ref[tc:tc + 8, :]

    xp_ref[8:8 + tc, :] = xr_ref[0]
    cw = cw_ref[...]
    xc = cb_ref[...] + cw[0:1] * xp_ref[5:5 + tc, :]
    xc = xc + cw[1:2] * xp_ref[6:6 + tc, :]
    xc = xc + cw[2:3] * xp_ref[7:7 + tc, :]
    xc = xc + cw[3:4] * xp_ref[8:8 + tc, :]
    _lru_terms(xc, wg_ref, ba_ref[...], bx_ref[...], lam_ref[...], a_ref, b_ref)

    def step(i, h):
        h = a_ref[pl.ds(i, 1), :] * h + b_ref[pl.ds(i, 1), :]
        h_ref[pl.ds(i, 1), :] = h
        return h

    h = lax.fori_loop(0, tc, step, hc_ref[0:1, :], unroll=8)
    hc_ref[0:1, :] = h
    out_ref[0] = (h_ref[...] * _gelu_tanh(yr_ref[0])).astype(out_ref.dtype)

    @pl.when(t == pl.num_programs(1) - 1)
    def _():
        hl_ref[0] = h


def _rglru_prompt(xy, conv_w, conv_b, wg, ba, bx, lam):
    B, T, _ = xy.shape
    tc = min(256, T)
    vec = lambda a: a.reshape(1, D_RNN)
    full = lambda shape: pl.BlockSpec(shape, lambda b, t: (0,) * len(shape))
    return pl.pallas_call(
        functools.partial(_rglru_prompt_body, tc=tc),
        grid=(B, T // tc),
        in_specs=[pl.BlockSpec((1, tc, D_RNN), lambda b, t: (b, t, 0)),
                  pl.BlockSpec((1, tc, D_RNN), lambda b, t: (b, t, 1)),
                  full((CONV_W, D_RNN)), full((1, D_RNN)), full((4, 256, 512)),
                  full((1, D_RNN)), full((1, D_RNN)), full((1, D_RNN))],
        out_specs=[pl.BlockSpec((1, tc, D_RNN), lambda b, t: (b, t, 0)),
                   pl.BlockSpec((1, 1, D_RNN), lambda b, t: (b, 0, 0))],
        out_shape=[jax.ShapeDtypeStruct((B, T, D_RNN), BF16),
                   jax.ShapeDtypeStruct((B, 1, D_RNN), F32)],
        scratch_shapes=[pltpu.VMEM((tc + 8, D_RNN), F32), pltpu.VMEM((tc, D_RNN), F32),
                        pltpu.VMEM((tc, D_RNN), F32), pltpu.VMEM((tc, D_RNN), F32),
                        pltpu.VMEM((8, D_RNN), F32)],
        compiler_params=_cparams(("parallel", "arbitrary")),
        name="rglru_prompt",
    )(xy, xy, conv_w, vec(conv_b), wg, vec(ba), vec(bx), vec(lam))


def _rglru_step_body(xr_ref, yr_ref, prev_ref, h0_ref, cw_ref, cb_ref, wg_ref, ba_ref, bx_ref, lam_ref,
                     out_ref, hn_ref, a_ref, b_ref):
    cw = cw_ref[...]
    xc = cb_ref[...] + cw[0:1] * prev_ref[0] + cw[1:2] * prev_ref[1] + cw[2:3] * prev_ref[2] + cw[3:4] * xr_ref[...]
    _lru_terms(xc, wg_ref, ba_ref[...], bx_ref[...], lam_ref[...], a_ref, b_ref)
    h = a_ref[...] * h0_ref[...] + b_ref[...]
    hn_ref[...] = h
    out_ref[...] = (h * _gelu_tanh(yr_ref[...])).astype(out_ref.dtype)


def _rglru_step(xy, prev, h0, conv_w, conv_b, wg, ba, bx, lam):
    R = xy.shape[0]
    vec = lambda a: a.reshape(1, D_RNN)
    full = lambda shape: pl.BlockSpec(shape, lambda i: (0,) * len(shape))
    return pl.pallas_call(
        _rglru_step_body,
        grid=(1,),
        in_specs=[pl.BlockSpec((R, D_RNN), lambda i: (0, 0)), pl.BlockSpec((R, D_RNN), lambda i: (0, 1)),
                  full((CONV_W - 1, R, D_RNN)), full((R, D_RNN)),
                  full((CONV_W, D_RNN)), full((1, D_RNN)), full((4, 256, 512)),
                  full((1, D_RNN)), full((1, D_RNN)), full((1, D_RNN))],
        out_specs=[full((R, D_RNN)), full((R, D_RNN))],
        out_shape=[jax.ShapeDtypeStruct((R, D_RNN), BF16), jax.ShapeDtypeStruct((R, D_RNN), F32)],
        scratch_shapes=[pltpu.VMEM((R, D_RNN), F32), pltpu.VMEM((R, D_RNN), F32)],
        compiler_params=_cparams(("arbitrary",)),
        name="rglru_step",
    )(xy, xy, prev, h0, conv_w, vec(conv_b), wg, vec(ba), vec(bx), vec(lam))


def _block_diag_gates(wa, wx):
    def bd(w):
        w4 = w.reshape(4, 4, LRU_BW, LRU_BW)
        return jnp.einsum('gaij,ab->gaibj', w4, jnp.eye(4, dtype=w.dtype)).reshape(4, 256, 256)
    return jnp.concatenate([bd(wa), bd(wx)], axis=-1).astype(BF16)


def _sortable_key(s):
    bits = pltpu.bitcast(s + 0.0, I32)
    return jnp.where(bits >= 0, bits, bits ^ jnp.int32(0x7FFFFFFF))


def _topk_threshold(count_ge, rows):
    cnt0 = count_ge(jnp.zeros((rows, 1), I32))
    pos = cnt0 >= TOPK
    base = jnp.where(pos, jnp.int32(0), jnp.int32(INT_MIN))
    cnt = jnp.where(pos, cnt0, jnp.int32(TOPK))

    def bit_body(i, carry):
        base, cnt = carry
        cand = base | jnp.left_shift(jnp.int32(1), jnp.int32(30) - i)
        c = count_ge(cand)
        ok = c >= TOPK
        return jnp.where(ok, cand, base), jnp.where(ok, c, cnt)

    return lax.fori_loop(0, 31, bit_body, (base, cnt))


def _attend_block(q_ref, k, v, mask, bias_of_head, m_ref, l_ref, acc_ref):
    for h in range(N_HEADS):
        c = h // (N_HEADS // N_KV_HEADS)
        hs = slice(h * HEAD_DIM, (h + 1) * HEAD_DIM)
        cs = slice(c * HEAD_DIM, (c + 1) * HEAD_DIM)
        s = lax.dot_general(q_ref[0, :, hs], k[:, cs], (((1,), (1,)), ((), ())),
                            preferred_element_type=F32)
        bias = bias_of_head(h)
        s = jnp.where(mask, s if bias is None else s + bias, NEG)
        m_old = m_ref[h]
        m_new = jnp.maximum(m_old, jnp.max(s, axis=-1, keepdims=True))
        alpha = jnp.exp2(m_old - m_new)
        p = jnp.exp2(s - m_new)
        l_ref[h] = alpha * l_ref[h] + jnp.sum(p, axis=-1, keepdims=True)
        acc_ref[:, hs] = alpha * acc_ref[:, hs] + jnp.dot(p.astype(BF16), v[:, cs],
                                                          preferred_element_type=F32)
        m_ref[h] = m_new


def _dsa_prompt_body(qi_ref, wi_ref, kit_ref, q_ref, kf_ref, vf_ref, kp_ref, vp_ref,
                     kd_ref, vd_ref, tbd_ref, tbp_ref, out_ref,
                     s_ref, thr_ref, m_ref, l_ref, acc_ref, *, tq, kb):
    qb = pl.program_id(1)
    st = pl.program_id(2)
    t0 = qb * tq
    far_len = jnp.maximum(qb - 1, 0) * tq
    n_far = (far_len + kb - 1) // kb

    @pl.when(st == 0)
    def _():
        m_ref[...] = jnp.full(m_ref.shape, NEG, F32)
        l_ref[...] = jnp.zeros_like(l_ref)
        acc_ref[...] = jnp.zeros_like(acc_ref)

        nkc = (t0 + tq + kb - 1) // kb
        wi = wi_ref[0]
        row = t0 + lax.broadcasted_iota(I32, (tq, kb), 0)
        lane = lax.broadcasted_iota(I32, (tq, kb), 1)

        def score_chunk(c, carry):
            off = pl.multiple_of(c * kb, kb)
            kic = kit_ref[0, :, pl.ds(off, kb)]
            acc = jnp.zeros((tq, kb), F32)
            for h in range(N_IDX_HEADS):
                s = jnp.dot(qi_ref[0, h], kic, preferred_element_type=F32)
                acc = acc + jnp.maximum(s, 0.0) * wi[:, h:h + 1]
            key = jnp.where(off + lane <= row, _sortable_key(acc), jnp.int32(INT_MIN))
            s_ref[:, pl.ds(off, kb)] = key
            return carry

        lax.fori_loop(0, nkc, score_chunk, 0)

        def count_ge(cand):
            def body(c, part):
                off = pl.multiple_of(c * kb, kb)
                ge = (s_ref[:, pl.ds(off, kb)] >= cand).astype(I32)
                for j in range(kb // 128):
                    part = part + ge[:, j * 128:(j + 1) * 128]
                return part
            part = lax.fori_loop(0, nkc, body, jnp.zeros((tq, 128), I32))
            return jnp.sum(part, axis=-1, keepdims=True)

        base, cnt_base = _topk_threshold(count_ge, tq)
        thr = jnp.maximum(base, jnp.int32(INT_MIN + 1))
        thr_ref[...] = thr

        tie = jnp.logical_and(base > INT_MIN, cnt_base > TOPK)

        @pl.when(jnp.max(tie.astype(I32)) > 0)
        def _():
            n_gt = count_ge(base + 1)
            need = jnp.where(tie, TOPK - n_gt, jnp.int32(2 ** 30)).astype(F32)
            sub = 256
            incl = (lax.broadcasted_iota(I32, (sub, sub), 0) <= lax.broadcasted_iota(I32, (sub, sub), 1)).astype(BF16)

            def tie_chunk(c, seen):
                for j in range(kb // sub):
                    off = pl.multiple_of(c * kb + j * sub, sub)
                    blk = s_ref[:, pl.ds(off, sub)]
                    eq = (blk == base).astype(F32)
                    rank = jnp.dot(eq.astype(BF16), incl, preferred_element_type=F32) + seen
                    demote = eq * (rank - need) > 0.0
                    s_ref[:, pl.ds(off, sub)] = jnp.where(demote, base - 1, blk)
                    seen = seen + jnp.sum(eq, axis=-1, keepdims=True)
                return seen

            lax.fori_loop(0, nkc, tie_chunk, jnp.zeros((tq, 1), F32))

        @pl.when(qb > 0)
        def _():
            off = pl.multiple_of(t0 - tq, tq)
            mask = s_ref[:, pl.ds(off, tq)] >= thr
            _attend_block(q_ref, kp_ref[0], vp_ref[0], mask, lambda h: tbp_ref[h], m_ref, l_ref, acc_ref)

        off = pl.multiple_of(t0, tq)
        mask = s_ref[:, pl.ds(off, tq)] >= thr
        _attend_block(q_ref, kd_ref[0], vd_ref[0], mask, lambda h: tbd_ref[h], m_ref, l_ref, acc_ref)

    @pl.when(jnp.logical_and(st > 0, st - 1 < n_far))
    def _():
        off = pl.multiple_of((st - 1) * kb, kb)
        col = off + lax.broadcasted_iota(I32, (tq, kb), 1)
        keys = jnp.where(col < far_len, s_ref[:, pl.ds(off, kb)], jnp.int32(INT_MIN))
        _attend_block(q_ref, kf_ref[0], vf_ref[0], keys >= thr_ref[...], lambda h: None, m_ref, l_ref, acc_ref)

    @pl.when(st == pl.num_programs(2) - 1)
    def _():
        for h in range(N_HEADS):
            hs = slice(h * HEAD_DIM, (h + 1) * HEAD_DIM)
            out_ref[0, :, hs] = (acc_ref[:, hs] / l_ref[h]).astype(out_ref.dtype)


def _t5_bucket(dist):
    n = jnp.maximum(dist, 0)
    max_exact = N_BUCKETS // 2
    nf = jnp.maximum(n, 1).astype(F32)
    large = max_exact + (jnp.log(nf / max_exact) / math.log(MAX_DISTANCE / max_exact)
                         * (N_BUCKETS - max_exact)).astype(I32)
    large = jnp.minimum(large, N_BUCKETS - 1)
    return jnp.where(n < max_exact, n, large)


_FAR_DIST = int(math.ceil((N_BUCKETS // 2) * (MAX_DISTANCE / (N_BUCKETS // 2)) **
                          ((N_BUCKETS - 1 - N_BUCKETS // 2) / (N_BUCKETS - N_BUCKETS // 2)))) + 1


def _dsa_prompt(q, k, v, qi_h, wi, kit, rel_bias):
    B, T, _ = q.shape
    tq = min(DSA_TQ, T)
    kb = min(DSA_KB, T)
    assert T % tq == 0 and T % kb == 0 and kb % tq == 0 and tq >= _FAR_DIST and T >= 4 * TOPK
    nq = T // tq
    n_far_max = max(1, ((nq - 2) * tq + kb - 1) // kb) if nq > 1 else 1
    d = jnp.arange(tq, dtype=I32)[:, None] - jnp.arange(tq, dtype=I32)[None, :]
    rel = (rel_bias - rel_bias[N_BUCKETS - 1]).astype(F32) * LOG2E
    tb_diag = rel[_t5_bucket(d)].transpose(2, 0, 1)
    tb_prev = rel[_t5_bucket(d + tq)].transpose(2, 0, 1)

    def far_idx(b, qb, st):
        far_len = jnp.maximum(qb - 1, 0) * tq
        n_far = (far_len + kb - 1) // kb
        return (b, jnp.clip(st - 1, 0, jnp.maximum(n_far - 1, 0)), 0)

    kv_far = pl.BlockSpec((1, kb, N_KV_HEADS * HEAD_DIM), far_idx)
    kv_prev = pl.BlockSpec((1, tq, N_KV_HEADS * HEAD_DIM), lambda b, qb, st: (b, jnp.maximum(qb - 1, 0), 0))
    kv_diag = pl.BlockSpec((1, tq, N_KV_HEADS * HEAD_DIM), lambda b, qb, st: (b, qb, 0))
    tile = pl.BlockSpec((N_HEADS, tq, tq), lambda b, qb, st: (0, 0, 0))
    return pl.pallas_call(
        functools.partial(_dsa_prompt_body, tq=tq, kb=kb),
        grid=(B, nq, n_far_max + 1),
        in_specs=[pl.BlockSpec((1, N_IDX_HEADS, tq, IDX_DIM), lambda b, qb, st: (b, 0, qb, 0)),
                  pl.BlockSpec((1, tq, N_IDX_HEADS), lambda b, qb, st: (b, qb, 0)),
                  pl.BlockSpec((1, IDX_DIM, T), lambda b, qb, st: (b, 0, 0)),
                  pl.BlockSpec((1, tq, N_HEADS * HEAD_DIM), lambda b, qb, st: (b, qb, 0)),
                  kv_far, kv_far, kv_prev, kv_prev, kv_diag, kv_diag, tile, tile],
        out_specs=pl.BlockSpec((1, tq, N_HEADS * HEAD_DIM), lambda b, qb, st: (b, qb, 0)),
        out_shape=jax.ShapeDtypeStruct((B, T, N_HEADS * HEAD_DIM), BF16),
        scratch_shapes=[pltpu.VMEM((tq, T), I32), pltpu.VMEM((tq, 1), I32),
                        pltpu.VMEM((N_HEADS, tq, 1), F32), pltpu.VMEM((N_HEADS, tq, 1), F32),
                        pltpu.VMEM((tq, N_HEADS * HEAD_DIM), F32)],
        compiler_params=_cparams(("parallel", "arbitrary", "arbitrary")),
        name="dsa_prompt",
    )(qi_h, wi, kit, q, k, v, k, v, k, v, tb_diag, tb_prev)


def _mem_attn_body(xq_ref, mk_ref, mv_ref, out_ref):
    scale = X_HEAD_DIM ** -0.5
    for h in range(X_HEADS):
        hs = slice(h * X_HEAD_PAD, (h + 1) * X_HEAD_PAD)
        s = lax.dot_general(xq_ref[0, :, hs], mk_ref[0, :, hs], (((1,), (1,)), ((), ())),
                            preferred_element_type=F32) * scale
        m = jnp.max(s, axis=-1, keepdims=True)
        p = jnp.exp(s - m)
        p = p / jnp.sum(p, axis=-1, keepdims=True)
        out_ref[0, :, hs] = jnp.dot(p.astype(BF16), mv_ref[0, :, hs],
                                    preferred_element_type=F32).astype(out_ref.dtype)


def _mem_attn(xq, mk, mv):
    B, T, W = xq.shape
    M = mk.shape[1]
    tm = min(512, T)
    return pl.pallas_call(
        _mem_attn_body,
        grid=(B, T // tm),
        in_specs=[pl.BlockSpec((1, tm, W), lambda b, i: (b, i, 0)),
                  pl.BlockSpec((1, M, W), lambda b, i: (b, 0, 0)),
                  pl.BlockSpec((1, M, W), lambda b, i: (b, 0, 0))],
        out_specs=pl.BlockSpec((1, tm, W), lambda b, i: (b, i, 0)),
        out_shape=jax.ShapeDtypeStruct((B, T, W), BF16),
        compiler_params=_cparams(("parallel", "parallel")),
        name="mem_attn",
    )(xq, mk, mv)


def _merge_body(x_ref, rnn_ref, att_ref, mem_ref, g_ref, wr_ref, wa_ref, wm_ref, wo_ref, gn_ref,
                x1_ref, hf_ref):
    mixed = _sigmoid(g_ref[:, 0:D_MODEL]) * jnp.dot(rnn_ref[...], wr_ref[...], preferred_element_type=F32)
    mixed = mixed + _sigmoid(g_ref[:, D_MODEL:2 * D_MODEL]) * jnp.dot(att_ref[...], wa_ref[...],
                                                                     preferred_element_type=F32)
    mixed = mixed + _sigmoid(g_ref[:, 2 * D_MODEL:3 * D_MODEL]) * jnp.dot(mem_ref[...], wm_ref[...],
                                                                         preferred_element_type=F32)
    x1 = x_ref[...] + jnp.dot(mixed.astype(BF16), wo_ref[...], preferred_element_type=F32)
    x1_ref[...] = x1
    ms = jnp.mean(x1 * x1, axis=-1, keepdims=True)
    hf_ref[...] = (x1 * lax.rsqrt(ms + EPS) * gn_ref[...]).astype(BF16)


def _merge(x, rnn, att, mem, g, wr, wa, wm, wo, gain):
    M = x.shape[0]
    tm = min(256, M)
    row = lambda w: pl.BlockSpec((tm, w), lambda i: (i, 0))
    full = lambda a: pl.BlockSpec(a.shape, lambda i: (0,) * a.ndim)
    gain = gain.reshape(1, D_MODEL)
    return pl.pallas_call(
        _merge_body,
        grid=(M // tm,),
        in_specs=[row(D_MODEL), row(rnn.shape[1]), row(att.shape[1]), row(mem.shape[1]), row(3 * D_MODEL),
                  full(wr), full(wa), full(wm), full(wo), full(gain)],
        out_specs=[row(D_MODEL), row(D_MODEL)],
        out_shape=[jax.ShapeDtypeStruct((M, D_MODEL), F32), jax.ShapeDtypeStruct((M, D_MODEL), BF16)],
        compiler_params=_cparams(("parallel",)),
        name="merge",
    )(x, rnn, att, mem, g, wr, wa, wm, wo, gain)


def _router_body(h_ref, w_ref, b_ref, pos_ref, gate_ref, cnt_ref, *, tm):
    logits = jnp.dot(h_ref[...], w_ref[...], preferred_element_type=F32) + b_ref[...]
    lane = lax.broadcasted_iota(I32, logits.shape, 1)
    vals, idxs = [], []
    for _ in range(TOP_K):
        m = jnp.max(logits, axis=-1, keepdims=True)
        i = jnp.min(jnp.where(logits == m, lane, jnp.int32(128)), axis=-1, keepdims=True)
        vals.append(m)
        idxs.append(i)
        logits = jnp.where(lane == i, -jnp.inf, logits)
    es = [jnp.exp(v - vals[0]) for v in vals]
    tot = es[0] + es[1] + es[2] + es[3]

    hots = [lane == i for i in idxs]
    hotf = [h.astype(F32) for h in hots]
    cnts = [jnp.sum(h, axis=0, keepdims=True) for h in hotf]
    c_i = (cnts[0] + cnts[1] + cnts[2] + cnts[3]).astype(I32)
    cpad = (((c_i + (SEG_ALIGN - 1)) // SEG_ALIGN) * SEG_ALIGN).astype(F32)
    r128 = lax.broadcasted_iota(I32, (128, 128), 0)
    c128 = lax.broadcasted_iota(I32, (128, 128), 1)
    loff = jnp.dot(jnp.broadcast_to(cpad, (8, 128)).astype(BF16), (r128 < c128).astype(BF16),
                   preferred_element_type=F32)[0:1]
    low = (lax.broadcasted_iota(I32, (tm, tm), 0) > lax.broadcasted_iota(I32, (tm, tm), 1)).astype(BF16)
    before = loff
    po = jnp.zeros(lane.shape, I32)
    go = jnp.zeros(lane.shape, F32)
    for k in range(TOP_K):
        pref = jnp.dot(low, hotf[k].astype(BF16), preferred_element_type=F32)
        pos = jnp.sum(jnp.where(hots[k], before + pref, 0.0), axis=-1, keepdims=True)
        before = before + cnts[k]
        po = jnp.where(lane == k, pos.astype(I32), po)
        go = jnp.where(lane == k, es[k] / tot, go)
    pos_ref[...] = po
    gate_ref[...] = go
    cnt_ref[0] = jnp.broadcast_to(c_i, (8, 128))


def _router(hf, rw, rb, tm):
    M = hf.shape[0]
    nt = M // tm
    return pl.pallas_call(
        functools.partial(_router_body, tm=tm),
        grid=(nt,),
        in_specs=[pl.BlockSpec((tm, D_MODEL), lambda i: (i, 0)),
                  pl.BlockSpec((D_MODEL, 128), lambda i: (0, 0)),
                  pl.BlockSpec((1, 128), lambda i: (0, 0))],
        out_specs=[pl.BlockSpec((tm, 128), lambda i: (i, 0)), pl.BlockSpec((tm, 128), lambda i: (i, 0)),
                   pl.BlockSpec((1, 8, 128), lambda i: (i, 0, 0))],
        out_shape=[jax.ShapeDtypeStruct((M, 128), I32), jax.ShapeDtypeStruct((M, 128), F32),
                   jax.ShapeDtypeStruct((nt, 8, 128), I32)],
        compiler_params=_cparams(("parallel",)),
        name="router",
    )(hf, rw, rb)


def _segment_copies(tile, n8_ref, loff_ref, gdst_ref, vm_ref, hbm_ref, sem, nbits, to_hbm, start):
    def body(e, carry):
        j = tile * N_EXPERTS + e
        n8, lo, gd = n8_ref[j], loff_ref[j], gdst_ref[j]
        for b in range(nbits):
            size = SEG_ALIGN << b
            off = (n8 & ((1 << b) - 1)) * SEG_ALIGN

            @pl.when(((n8 >> b) & 1) == 1)
            def _():
                v = vm_ref.at[pl.ds(pl.multiple_of(lo + off, SEG_ALIGN), size)]
                h = hbm_ref.at[pl.ds(pl.multiple_of(gd + off, SEG_ALIGN), size)]
                cp = pltpu.make_async_copy(v, h, sem) if to_hbm else pltpu.make_async_copy(h, v, sem)
                if start:
                    cp.start()
                else:
                    cp.wait()
        return carry

    lax.fori_loop(0, N_EXPERTS, body, 0)


def _dispatch_body(n8_ref, loff_ref, gdst_ref, pos_ref, h_ref, xinit_ref, xbuf_ref, xs_ref, sem, *, tm, rows, nbits):
    del xinit_ref
    tile = pl.program_id(0)
    pos_t = pos_ref[...].astype(F32).T
    sub = lax.broadcasted_iota(I32, (rows, tm), 0).astype(F32)
    perm = jnp.zeros((rows, tm), F32)
    for k in range(TOP_K):
        perm = perm + jnp.where(sub == pos_t[k:k + 1, :], 1.0, 0.0)
    xs_ref[...] = jnp.dot(perm.astype(BF16), h_ref[...], preferred_element_type=F32)
    _segment_copies(tile, n8_ref, loff_ref, gdst_ref, xs_ref, xbuf_ref, sem, nbits, True, True)
    _segment_copies(tile, n8_ref, loff_ref, gdst_ref, xs_ref, xbuf_ref, sem, nbits, True, False)


def _combine_body(n8_ref, loff_ref, gdst_ref, pos_ref, gate_ref, x1_ref, ybuf_ref, out_ref, ys_ref, sem,
                  *, tm, rows, nbits):
    tile = pl.program_id(0)

    @pl.when(tile == 0)
    def _():
        ys_ref[...] = jnp.zeros_like(ys_ref)

    _segment_copies(tile, n8_ref, loff_ref, gdst_ref, ys_ref, ybuf_ref, sem, nbits, False, True)
    lane = lax.broadcasted_iota(I32, (tm, rows), 1)
    pos = pos_ref[...]
    gate = gate_ref[...]
    g = jnp.zeros((tm, rows), F32)
    for k in range(TOP_K):
        g = g + jnp.where(lane == pos[:, k:k + 1], gate[:, k:k + 1], 0.0)
    g_hi = g.astype(BF16)
    g_lo = (g - g_hi.astype(F32)).astype(BF16)
    _segment_copies(tile, n8_ref, loff_ref, gdst_ref, ys_ref, ybuf_ref, sem, nbits, False, False)
    ys = ys_ref[...]
    y_hi = ys.astype(BF16)
    y_lo = (ys - y_hi.astype(F32)).astype(BF16)
    acc = jnp.dot(g_hi, y_hi, preferred_element_type=F32)
    acc = acc + jnp.dot(g_hi, y_lo, preferred_element_type=F32)
    acc = acc + jnp.dot(g_lo, y_hi, preferred_element_type=F32)
    out_ref[...] = x1_ref[...] + acc


def _expert_body(be_ref, nu_ref, x_ref, wgu_ref, bgu_ref, wdn_ref, bdn_ref, y_ref):
    @pl.when(pl.program_id(0) < nu_ref[0])
    def _():
        hgu = jnp.dot(x_ref[...].astype(BF16), wgu_ref[0], preferred_element_type=F32) + bgu_ref[0]
        gate = jnp.minimum(hgu[:, :D_FF], SWIGLU_LIMIT)
        up = jnp.clip(hgu[:, D_FF:], -SWIGLU_LIMIT, SWIGLU_LIMIT)
        act = (up + 1.0) * gate * _sigmoid(SWIGLU_ALPHA * gate)
        y_ref[...] = jnp.dot(act.astype(BF16), wdn_ref[0], preferred_element_type=F32) + bdn_ref[0]

    @pl.when(pl.program_id(0) >= nu_ref[0])
    def _():
        y_ref[...] = jnp.zeros_like(y_ref)


def _experts(xbuf, blk_e, n_used, wgu, bgu, wdn, bdn):
    nb = blk_e.shape[0]
    rb = xbuf.shape[0] // nb
    last = lambda i, nu: jnp.minimum(i, jnp.maximum(nu[0] - 1, 0))
    grid_spec = pltpu.PrefetchScalarGridSpec(
        num_scalar_prefetch=2,
        grid=(nb,),
        in_specs=[pl.BlockSpec((rb, D_MODEL), lambda i, be, nu: (last(i, nu), 0)),
                  pl.BlockSpec((1, D_MODEL, 2 * D_FF), lambda i, be, nu: (be[i], 0, 0)),
                  pl.BlockSpec((1, 1, 2 * D_FF), lambda i, be, nu: (be[i], 0, 0)),
                  pl.BlockSpec((1, D_FF, D_MODEL), lambda i, be, nu: (be[i], 0, 0)),
                  pl.BlockSpec((1, 1, D_MODEL), lambda i, be, nu: (be[i], 0, 0))],
        out_specs=pl.BlockSpec((rb, D_MODEL), lambda i, be, nu: (i, 0)),
    )
    return pl.pallas_call(
        _expert_body,
        grid_spec=grid_spec,
        out_shape=jax.ShapeDtypeStruct((nb * rb, D_MODEL), F32),
        compiler_params=_cparams(("arbitrary",)),
        name="experts",
    )(blk_e, n_used, xbuf, wgu, bgu.reshape(N_EXPERTS, 1, 2 * D_FF), wdn, bdn.reshape(N_EXPERTS, 1, D_MODEL))


def _moe(hf, x1, W):
    M = hf.shape[0]
    tm = min(MOE_TM, M)
    nt = M // tm
    assert M % tm == 0 and tm * TOP_K // SEG_ALIGN <= 256
    pos, gates, cnt = _router(hf, W['router_w'], W['router_b'], tm)

    rb = MOE_RB if M * TOP_K >= N_EXPERTS * MOE_RB else 128
    cpad = (cnt[:, 0, :N_EXPERTS] + (SEG_ALIGN - 1)) // SEG_ALIGN * SEG_ALIGN
    loff = jnp.cumsum(cpad, axis=1) - cpad
    region = (jnp.sum(cpad, axis=0) + rb - 1) // rb * rb
    pend = jnp.cumsum(region)
    gdst = (pend - region)[None, :] + jnp.cumsum(cpad, axis=0) - cpad
    max_rows = M * TOP_K + nt * N_EXPERTS * (SEG_ALIGN - 1)
    nb = (max_rows + rb - 1) // rb + N_EXPERTS
    blk_e = jnp.minimum(jnp.searchsorted(pend, jnp.arange(nb, dtype=I32) * rb, side='right'),
                        N_EXPERTS - 1).astype(I32)
    n_used = (pend[-1] // rb).astype(I32).reshape(1)
    plan = ((cpad // SEG_ALIGN).reshape(-1).astype(I32), loff.reshape(-1).astype(I32),
            gdst.reshape(-1).astype(I32))

    rows = (tm * TOP_K + N_EXPERTS * (SEG_ALIGN - 1) + 127) // 128 * 128
    nbits = (tm * TOP_K // SEG_ALIGN).bit_length()
    any_spec = pl.BlockSpec(memory_space=pl.ANY)
    row = lambda w: pl.BlockSpec((tm, w), lambda i, *_: (i, 0))
    xbuf = pl.pallas_call(
        functools.partial(_dispatch_body, tm=tm, rows=rows, nbits=nbits),
        grid_spec=pltpu.PrefetchScalarGridSpec(
            num_scalar_prefetch=3, grid=(nt,),
            in_specs=[row(128), row(D_MODEL), any_spec],
            out_specs=any_spec,
            scratch_shapes=[pltpu.VMEM((rows, D_MODEL), F32), pltpu.SemaphoreType.DMA(())]),
        out_shape=jax.ShapeDtypeStruct((nb * rb, D_MODEL), F32),
        input_output_aliases={5: 0},
        compiler_params=_cparams(("arbitrary",)),
        name="moe_dispatch",
    )(*plan, pos, hf, jnp.zeros((nb * rb, D_MODEL), F32))
    ybuf = _experts(xbuf, blk_e, n_used, W['exp_w_gu'], W['exp_b_gu'], W['exp_w_down'], W['exp_b_down'])
    return pl.pallas_call(
        functools.partial(_combine_body, tm=tm, rows=rows, nbits=nbits),
        grid_spec=pltpu.PrefetchScalarGridSpec(
            num_scalar_prefetch=3, grid=(nt,),
            in_specs=[row(128), row(128), row(D_MODEL), any_spec],
            out_specs=row(D_MODEL),
            scratch_shapes=[pltpu.VMEM((rows, D_MODEL), F32), pltpu.SemaphoreType.DMA(())]),
        out_shape=jax.ShapeDtypeStruct((M, D_MODEL), F32),
        compiler_params=_cparams(("arbitrary",)),
        name="moe_combine",
    )(*plan, pos, gates, x1, ybuf)


_NT = (((1,), (1,)), ((), ()))


def _dsa_step_body(pt_ref, qi_ref, wi_ref, kin_ref, q_ref, kn_ref, vn_ref, tbl_ref, bnew_ref, hsel_ref, exp_ref,
                   *refs, g, n_pages):
    del pt_ref
    ki_refs, k_refs, v_refs = refs[:g], refs[g:2 * g], refs[2 * g:3 * g]
    out_ref, s_ref, thr_ref, knew_ref, m_ref, l_ref, acc_ref = refs[3 * g:]
    j = pl.program_id(1)
    ns = n_pages // g
    qi = qi_ref[0]
    wi = wi_ref[0]

    @pl.when(j < ns)
    def _():
        for t in range(g):
            s = lax.dot_general(qi, ki_refs[t][0].astype(BF16), _NT, preferred_element_type=F32)
            row = jnp.sum(jnp.maximum(s, 0.0) * wi, axis=0, keepdims=True)
            s_ref[pl.ds(j * g + t, 1), :] = _sortable_key(row)

    @pl.when(j == ns - 1)
    def _():
        sn = jnp.sum(qi.astype(F32) * kin_ref[0].astype(BF16).astype(F32), axis=-1, keepdims=True)
        knew = _sortable_key(jnp.sum(jnp.maximum(sn, 0.0) * wi, axis=0, keepdims=True))

        def total(x):
            return jnp.sum(jnp.sum(x, axis=0, keepdims=True), axis=1, keepdims=True)

        def count_ge(cand):
            return total((s_ref[...] >= cand).astype(I32)) + (knew >= cand).astype(I32)

        base, cnt = _topk_threshold(count_ge, 1)
        thr_ref[...] = jnp.maximum(base, jnp.int32(INT_MIN + 1))
        knew_ref[...] = knew
        tie = jnp.logical_and(base > INT_MIN, cnt > TOPK)

        @pl.when(jnp.max(tie.astype(I32)) > 0)
        def _():
            need = (TOPK - count_ge(base + 1)).astype(F32)
            keys = s_ref[...]
            eq = (keys == base).astype(F32)
            incl = (lax.broadcasted_iota(I32, (128, 128), 0) <= lax.broadcasted_iota(I32, (128, 128), 1))
            lane_rank = jnp.dot(eq.astype(BF16), incl.astype(BF16), preferred_element_type=F32)
            low = (lax.broadcasted_iota(I32, (n_pages, n_pages), 0) > lax.broadcasted_iota(I32, (n_pages, n_pages), 1))
            row_tot = jnp.broadcast_to(jnp.sum(eq, axis=1, keepdims=True), eq.shape)
            row_off = jnp.dot(low.astype(BF16), row_tot.astype(BF16), preferred_element_type=F32)
            s_ref[...] = jnp.where(eq * (lane_rank + row_off - need) > 0.0, base - 1, keys)
            new_demoted = jnp.logical_and(knew == base, total(eq) + 1.0 > need)
            knew_ref[...] = jnp.where(new_demoted, base - 1, knew)

        m_ref[...] = jnp.full(m_ref.shape, NEG, F32)
        l_ref[...] = jnp.zeros_like(l_ref)
        acc_ref[...] = jnp.zeros_like(acc_ref)

    @pl.when(j >= ns)
    def _():
        q8 = q_ref[0]
        p0 = pl.multiple_of((j - ns) * g, g)
        sel = (s_ref[pl.ds(p0, g), :] >= thr_ref[...]).astype(F32).astype(BF16)
        sel = jnp.dot(sel, exp_ref[...], preferred_element_type=F32)
        blocks = []
        for t in range(g):
            s = lax.dot_general(q8, k_refs[t][...].astype(BF16), _NT, preferred_element_type=F32)
            if t == g - 1:
                s = s + jnp.where(j == 2 * ns - 1, tbl_ref[...], 0.0)
            blocks.append(jnp.where(sel[t:t + 1, :] * hsel_ref[...] > 0.5, s, NEG))
        s_all = jnp.concatenate(blocks, axis=-1)
        m_old = m_ref[...]
        m_new = jnp.maximum(m_old, jnp.max(s_all, axis=-1, keepdims=True))
        alpha = jnp.exp2(m_old - m_new)
        pr = jnp.exp2(s_all - m_new)
        l_ref[...] = alpha * l_ref[...] + jnp.sum(pr, axis=-1, keepdims=True)
        acc = alpha * acc_ref[...]
        w = PAGE_SIZE * N_KV_HEADS
        for t in range(g):
            acc = acc + jnp.dot(pr[:, t * w:(t + 1) * w].astype(BF16), v_refs[t][...].astype(BF16),
                                preferred_element_type=F32)
        acc_ref[...] = acc
        m_ref[...] = m_new

    @pl.when(j == 2 * ns - 1)
    def _():
        kn = kn_ref[0].astype(BF16).astype(F32)
        s_new = jnp.sum(q_ref[0].astype(F32) * kn, axis=-1, keepdims=True) + bnew_ref[...]
        s_new = jnp.where(knew_ref[...] >= thr_ref[...], s_new, NEG)
        m_old = m_ref[...]
        m_new = jnp.maximum(m_old, s_new)
        alpha = jnp.exp2(m_old - m_new)
        pn = jnp.exp2(s_new - m_new)
        l = alpha * l_ref[...] + pn
        o = (alpha * acc_ref[...] + pn * vn_ref[0].astype(BF16).astype(F32)) / l
        out_ref[0] = o.astype(out_ref.dtype)


def _dsa_step(q, k_new, v_new, qi, wi, ki_new, cache_k, cache_v, cache_kidx, page_table, rel_bias):
    Bd, n_pages = page_table.shape
    past = n_pages * PAGE_SIZE
    g = min(DEC_G, n_pages)
    assert n_pages % g == 0 and PAGE_SIZE >= _FAR_DIST and TOPK <= (past + 1) // 4
    ns = n_pages // g
    n_phys = cache_k.shape[0]
    kvw = N_KV_HEADS * HEAD_DIM
    rel = (rel_bias - rel_bias[N_BUCKETS - 1]).astype(F32) * LOG2E
    pw = PAGE_SIZE * N_KV_HEADS
    col_tok = jnp.arange(pw, dtype=I32) // N_KV_HEADS
    col_kvh = jnp.arange(pw, dtype=I32) % N_KV_HEADS
    kvh_of_head = jnp.arange(N_HEADS, dtype=I32) // (N_HEADS // N_KV_HEADS)
    tbl = rel[_t5_bucket(PAGE_SIZE - col_tok)].T
    bnew = rel[_t5_bucket(jnp.zeros((1,), I32))].T
    hsel = (kvh_of_head[:, None] == col_kvh[None, :]).astype(F32)
    expand = (jnp.arange(PAGE_SIZE, dtype=I32)[:, None] == col_tok[None, :]).astype(BF16)
    kn8 = k_new.reshape(Bd, N_KV_HEADS, HEAD_DIM)[:, kvh_of_head]
    vn8 = v_new.reshape(Bd, N_KV_HEADS, HEAD_DIM)[:, kvh_of_head]

    per_seq = lambda shape: pl.BlockSpec((1,) + shape, lambda b, j, pt: (b, 0, 0))
    full = lambda shape: pl.BlockSpec(shape, lambda b, j, pt: (0, 0))

    def ki_page(t):
        return pl.BlockSpec((1, PAGE_SIZE, IDX_DIM),
                            lambda b, j, pt: (pt[b * n_pages + jnp.where(j < ns, j * g + t, (ns - 1) * g + t)], 0, 0))

    def kv_page(t):
        return pl.BlockSpec((pw, HEAD_DIM),
                            lambda b, j, pt: (pt[b * n_pages + jnp.where(j < ns, t, (j - ns) * g + t)], 0))

    grid_spec = pltpu.PrefetchScalarGridSpec(
        num_scalar_prefetch=1, grid=(Bd, 2 * ns),
        in_specs=[per_seq((N_IDX_HEADS, IDX_DIM)), per_seq((N_IDX_HEADS, 1)), per_seq((1, IDX_DIM)),
                  per_seq((N_HEADS, HEAD_DIM)), per_seq((N_HEADS, HEAD_DIM)), per_seq((N_HEADS, HEAD_DIM)),
                  full((N_HEADS, pw)), full((N_HEADS, 1)), full((N_HEADS, pw)), full((PAGE_SIZE, pw))]
                 + [ki_page(t) for t in range(g)] + [kv_page(t) for t in range(g)] + [kv_page(t) for t in range(g)],
        out_specs=per_seq((N_HEADS, HEAD_DIM)),
        scratch_shapes=[pltpu.VMEM((n_pages, PAGE_SIZE), I32), pltpu.VMEM((1, 1), I32), pltpu.VMEM((1, 1), I32),
                        pltpu.VMEM((N_HEADS, 1), F32), pltpu.VMEM((N_HEADS, 1), F32),
                        pltpu.VMEM((N_HEADS, HEAD_DIM), F32)])
    ck = cache_k.reshape(n_phys * pw, HEAD_DIM)
    cv = cache_v.reshape(n_phys * pw, HEAD_DIM)
    out = pl.pallas_call(
        functools.partial(_dsa_step_body, g=g, n_pages=n_pages),
        grid_spec=grid_spec,
        out_shape=jax.ShapeDtypeStruct((Bd, N_HEADS, HEAD_DIM), BF16),
        compiler_params=_cparams(("arbitrary", "arbitrary")),
        name="dsa_step",
    )(page_table.reshape(-1), qi.reshape(Bd, N_IDX_HEADS, IDX_DIM), wi.reshape(Bd, N_IDX_HEADS, 1),
      ki_new.reshape(Bd, 1, IDX_DIM), q.reshape(Bd, N_HEADS, HEAD_DIM), kn8, vn8, tbl, bnew, hsel, expand,
      *([cache_kidx] * g), *([ck] * g), *([cv] * g))
    return out.reshape(Bd, N_HEADS * HEAD_DIM)


def _mem_step_body(xq_ref, mk_ref, mv_ref, sel_ref, out_ref):
    s = lax.dot_general(xq_ref[0], mk_ref[0].astype(BF16), _NT, preferred_element_type=F32) * X_HEAD_DIM ** -0.5
    p = jnp.exp(s - jnp.max(s, axis=-1, keepdims=True))
    p = p / jnp.sum(p, axis=-1, keepdims=True)
    r = jnp.dot(p.astype(BF16), mv_ref[0].astype(BF16), preferred_element_type=F32)
    out_ref[0] = jnp.sum(r * sel_ref[...], axis=0, keepdims=True).astype(out_ref.dtype)


def _mem_step(xq, mem_k, mem_v):
    Bd, M = mem_k.shape[:2]
    w = X_HEADS * X_HEAD_DIM
    eye = jnp.eye(X_HEADS, dtype=xq.dtype)
    xbd = (xq[:, :, None, :] * eye[None, :, :, None]).reshape(Bd, X_HEADS, w)
    xbd = jnp.pad(xbd, ((0, 0), (0, 8 - X_HEADS), (0, 0)))
    sel = jnp.pad(jnp.repeat(jnp.eye(X_HEADS, dtype=F32), X_HEAD_DIM, axis=1), ((0, 8 - X_HEADS), (0, 0)))
    out = pl.pallas_call(
        _mem_step_body,
        grid=(Bd,),
        in_specs=[pl.BlockSpec((1, 8, w), lambda b: (b, 0, 0)), pl.BlockSpec((1, M, w), lambda b: (b, 0, 0)),
                  pl.BlockSpec((1, M, w), lambda b: (b, 0, 0)), pl.BlockSpec((8, w), lambda b: (0, 0))],
        out_specs=pl.BlockSpec((1, 1, w), lambda b: (b, 0, 0)),
        out_shape=jax.ShapeDtypeStruct((Bd, 1, w), BF16),
        compiler_params=_cparams(("parallel",)),
        name="mem_step",
    )(xbd, mem_k.reshape(Bd, M, w), mem_v.reshape(Bd, M, w), sel)
    return out.reshape(Bd, w)


def _pad_heads_cols(w):
    K = w.shape[0]
    w = w.reshape(K, X_HEADS, X_HEAD_DIM)
    return jnp.pad(w, ((0, 0), (0, 0), (0, X_HEAD_PAD - X_HEAD_DIM))).reshape(K, X_HEADS * X_HEAD_PAD)


def _pad_heads_vec(g):
    return jnp.pad(g, (0, X_HEAD_PAD - X_HEAD_DIM))


def _prep_weights(P):
    W = {}
    w_in = P['w_in'].astype(BF16)
    widths = (D_RNN, D_RNN, N_HEADS * HEAD_DIM, N_KV_HEADS * HEAD_DIM, N_KV_HEADS * HEAD_DIM,
              N_IDX_HEADS * IDX_DIM, IDX_DIM, N_IDX_HEADS, X_HEADS * X_HEAD_DIM, 3 * D_MODEL)
    offs = np.cumsum((0,) + widths)
    seg = lambda i: w_in[:, offs[i]:offs[i + 1]]
    W['w_xy'] = w_in[:, offs[0]:offs[2]]
    W['w_q'], W['w_k'], W['w_v'], W['w_qi'] = seg(2), seg(3), seg(4), seg(5)
    W['w_kiwi'] = jnp.pad(w_in[:, offs[6]:offs[8]], ((0, 0), (0, 128 - IDX_DIM - N_IDX_HEADS)))
    W['w_xq'] = _pad_heads_cols(seg(8))
    W['w_g'] = seg(9)
    W['q_gain'] = jnp.tile(P['q_norm'], N_HEADS)
    W['q_gain_log2'] = W['q_gain'] * (HEAD_DIM ** -0.5 * LOG2E)
    W['k_gain'] = jnp.tile(P['k_norm'], N_KV_HEADS)
    W['xq_gain'] = jnp.tile(_pad_heads_vec(P['xq_norm']), X_HEADS)
    W['xk_gain'] = jnp.tile(_pad_heads_vec(P['xk_norm']), X_HEADS)
    wmk, wmv = jnp.split(P['w_mem_kv'].astype(BF16), 2, axis=-1)
    W['w_mem_k'] = _pad_heads_cols(wmk)
    W['w_mem_v'] = _pad_heads_cols(wmv)
    W['wg_lru'] = _block_diag_gates(P['lru_wa'], P['lru_wx'])
    W['w_br_rnn'] = P['w_br_rnn'].astype(BF16)
    W['w_br_attn'] = P['w_br_attn'].astype(BF16)
    W['w_br_mem'] = _pad_heads_cols(P['w_br_mem'].astype(BF16).T).T
    W['w_out'] = P['w_out'].astype(BF16)
    W['router_w'] = jnp.pad(P['router_w'].astype(BF16), ((0, 0), (0, 128 - N_EXPERTS)))
    W['router_b'] = jnp.pad(P['router_b'].astype(F32), (0, 128 - N_EXPERTS), constant_values=-1e30).reshape(1, 128)
    W['exp_w_gu'] = P['exp_w_gu'].astype(BF16)
    W['exp_w_down'] = P['exp_w_down'].astype(BF16)
    for name in ('norm_mix', 'conv_w', 'conv_b', 'lru_ba', 'lru_bx', 'lru_lambda', 'rel_bias', 'mem_norm',
                 'norm_ffn', 'exp_b_gu', 'exp_b_down'):
        W[name] = P[name]
    return W


def _unpad_heads(a):
    return a.reshape(a.shape[:-1] + (X_HEADS, X_HEAD_PAD))[..., :X_HEAD_DIM]


def _in_proj(x2, W, q_gain):
    g = W['norm_mix']
    o = {}
    (o['xy'],) = _norm_proj(x2, g, W['w_xy'], [F32], tn=1024)
    (o['q'],) = _norm_proj(x2, g, W['w_q'], [BF16], tn=512, mode="headnorm", head_gain=q_gain)
    o['k'], o['k16'] = _norm_proj(x2, g, W['w_k'], [F32, BF16], tn=512, mode="headnorm", head_gain=W['k_gain'])
    o['v'], o['v16'] = _norm_proj(x2, g, W['w_v'], [F32, BF16], tn=512)
    (o['qi'],) = _norm_proj(x2, g, W['w_qi'], [BF16], tn=512)
    (o['kiwi'],) = _norm_proj(x2, g, W['w_kiwi'], [F32], tn=128)
    (o['xq'],) = _norm_proj(x2, g, W['w_xq'], [BF16], tn=512, mode="headnorm", head_gain=W['xq_gain'],
                            hd_pad=X_HEAD_PAD, hd_true=X_HEAD_DIM)
    (o['g'],) = _norm_proj(x2, g, W['w_g'], [F32], tn=1024)
    return o


def kernel(x_prompt, x_sample, cache_k, cache_v, cache_kidx, cache_mem_k, cache_mem_v, state_conv, state_rglru, page_table, mem_prompt, norm_mix, w_in, conv_w, conv_b, lru_wa, lru_ba, lru_wx, lru_bx, lru_lambda, q_norm, k_norm, rel_bias, mem_norm, w_mem_kv, xq_norm, xk_norm, w_br_rnn, w_br_attn, w_br_mem, w_out, norm_ffn, router_w, router_b, exp_w_gu, exp_b_gu, exp_w_down, exp_b_down):
    W = _prep_weights(dict(norm_mix=norm_mix, w_in=w_in, conv_w=conv_w, conv_b=conv_b, lru_wa=lru_wa,
                           lru_ba=lru_ba, lru_wx=lru_wx, lru_bx=lru_bx, lru_lambda=lru_lambda, q_norm=q_norm,
                           k_norm=k_norm, rel_bias=rel_bias, mem_norm=mem_norm, w_mem_kv=w_mem_kv,
                           xq_norm=xq_norm, xk_norm=xk_norm, w_br_rnn=w_br_rnn, w_br_attn=w_br_attn,
                           w_br_mem=w_br_mem, w_out=w_out, norm_ffn=norm_ffn, router_w=router_w,
                           router_b=router_b, exp_w_gu=exp_w_gu, exp_b_gu=exp_b_gu, exp_w_down=exp_w_down,
                           exp_b_down=exp_b_down))
    lru = (W['conv_w'], W['conv_b'], W['wg_lru'], W['lru_ba'], W['lru_bx'], W['lru_lambda'])

    B, T, D = x_prompt.shape
    xp2 = x_prompt.reshape(B * T, D)
    memp2 = mem_prompt.reshape(B * N_MEM, D)
    mk_pad, mk16 = _norm_proj(memp2, W['mem_norm'], W['w_mem_k'], [F32, BF16], tn=512, mode="headnorm",
                              head_gain=W['xk_gain'], hd_pad=X_HEAD_PAD, hd_true=X_HEAD_DIM)
    mv_pad, mv16 = _norm_proj(memp2, W['mem_norm'], W['w_mem_v'], [F32, BF16], tn=512)
    mk_p = _unpad_heads(mk_pad).reshape(B, N_MEM, X_HEADS, X_HEAD_DIM)
    mv_p = _unpad_heads(mv_pad).reshape(B, N_MEM, X_HEADS, X_HEAD_DIM)

    o = _in_proj(xp2, W, W['q_gain_log2'])
    xy = o['xy'].reshape(B, T, 2 * D_RNN)
    rnn_out, rg_p = _rglru_prompt(xy, *lru)
    conv_p = xy[:, T - (CONV_W - 1):, :D_RNN]
    ki_p = o['kiwi'][:, :IDX_DIM].reshape(B, T, IDX_DIM)
    wi_p = o['kiwi'][:, IDX_DIM:IDX_DIM + N_IDX_HEADS].reshape(B, T, N_IDX_HEADS)
    qi_h = o['qi'].reshape(B, T, N_IDX_HEADS, IDX_DIM).transpose(0, 2, 1, 3)
    kit = ki_p.astype(BF16).transpose(0, 2, 1)
    attn_out = _dsa_prompt(o['q'].reshape(B, T, -1), o['k16'].reshape(B, T, -1), o['v16'].reshape(B, T, -1),
                           qi_h, wi_p, kit, W['rel_bias'])
    mem_out = _mem_attn(o['xq'].reshape(B, T, -1), mk16.reshape(B, N_MEM, -1), mv16.reshape(B, N_MEM, -1))
    x1, hf = _merge(xp2, rnn_out.reshape(B * T, -1), attn_out.reshape(B * T, -1), mem_out.reshape(B * T, -1),
                    o['g'], W['w_br_rnn'], W['w_br_attn'], W['w_br_mem'], W['w_out'], W['norm_ffn'])
    y_prompt = _moe(hf, x1, W).reshape(B, T, D)
    k_p = o['k'].reshape(B, T, N_KV_HEADS, HEAD_DIM)
    v_p = o['v'].reshape(B, T, N_KV_HEADS, HEAD_DIM)

    Bd, S, _ = x_sample.shape
    xs2 = x_sample.reshape(Bd * S, D)
    os_ = _in_proj(xs2, W, W['q_gain_log2'])
    rnn_s, rg_s = _rglru_step(os_['xy'], state_conv.transpose(1, 0, 2), state_rglru, *lru)
    conv_s = jnp.concatenate([state_conv[:, 1:], os_['xy'][:, None, :D_RNN]], axis=1)
    k_s = os_['k'].reshape(Bd, S, N_KV_HEADS, HEAD_DIM)
    v_s = os_['v'].reshape(Bd, S, N_KV_HEADS, HEAD_DIM)
    ki_s = os_['kiwi'][:, :IDX_DIM].reshape(Bd, S, IDX_DIM)
    wi_s = os_['kiwi'][:, IDX_DIM:IDX_DIM + N_IDX_HEADS].reshape(Bd, S, N_IDX_HEADS)
    assert S == 1
    attn_s = _dsa_step(os_['q'], os_['k'], os_['v'], os_['qi'], wi_s.reshape(Bd, -1),
                       ki_s.reshape(Bd, -1), cache_k, cache_v, cache_kidx, page_table, W['rel_bias'])
    mem_s = _mem_step(_unpad_heads(os_['xq']), cache_mem_k, cache_mem_v)
    mem_s = jnp.pad(mem_s.reshape(Bd, X_HEADS, X_HEAD_DIM),
                    ((0, 0), (0, 0), (0, X_HEAD_PAD - X_HEAD_DIM))).reshape(Bd, -1)
    x1s, hfs = _merge(xs2, rnn_s, attn_s, mem_s, os_['g'],
                      W['w_br_rnn'], W['w_br_attn'], W['w_br_mem'], W['w_out'], W['norm_ffn'])
    y_sample = _moe(hfs, x1s, W).reshape(Bd, S, D)

    return (y_prompt, y_sample, k_p, v_p, ki_p, mk_p, mv_p, conv_p, rg_p.reshape(B, D_RNN),
            k_s, v_s, ki_s, conv_s, rg_s)
```
